```python
import jax, jax.numpy as jnp
from jax import lax
import numpy as np

D_MODEL = 1024
BATCH = 16
SEQ = 2048
DEPTH = 4

NUM_MIXERS = 4
RMS_EPS = 1e-6
ROPE_THETA = 500000.0
ATTN_BLOCK = 128

SSM_D_INNER = 2 * D_MODEL
SSM_HEAD_DIM = 64
SSM_N_HEADS = SSM_D_INNER // SSM_HEAD_DIM
SSM_D_STATE = 128
SSM_N_GROUPS = 8
SSM_CONV = 4
SSM_CHUNK = 128
SSM_BC_DIM = SSM_N_GROUPS * SSM_D_STATE
SSM_CONV_DIM = SSM_D_INNER + 2 * SSM_BC_DIM
SSM_IN_COLS = SSM_D_INNER + SSM_CONV_DIM + SSM_N_HEADS
SSM_DT_MIN = 0.001
SSM_DT_MAX = 0.1

MLA_N_HEADS = 16
MLA_Q_LORA = 3 * D_MODEL // 8
MLA_KV_LORA = D_MODEL // 4
MLA_NOPE_DIM = 128
MLA_ROPE_DIM = 64
MLA_V_DIM = 128
MLA_QK_DIM = MLA_NOPE_DIM + MLA_ROPE_DIM
MLA_WIDTH = MLA_N_HEADS * MLA_V_DIM
MLA_IN_COLS = MLA_Q_LORA + MLA_KV_LORA + MLA_ROPE_DIM + MLA_WIDTH

FOX_N_HEADS = 16
FOX_HEAD_DIM = 128
FOX_WIDTH = FOX_N_HEADS * FOX_HEAD_DIM
FOX_IN_COLS = 3 * FOX_WIDTH + FOX_N_HEADS + FOX_WIDTH
FOX_F_BIAS_MEAN = 3.0

DIL_CONFIGS = ((128, 1), (512, 4), (2048, 16))
DIL_N_HEADS = 8
DIL_HEAD_DIM = 128
DIL_WIDTH = DIL_N_HEADS * DIL_HEAD_DIM
DIL_ROPE_DIM = DIL_HEAD_DIM // 4
DIL_IN_COLS = 3 * len(DIL_CONFIGS) * DIL_WIDTH + DIL_WIDTH

N_SSM_LAYERS = (DEPTH + 3) // NUM_MIXERS
N_MLA_LAYERS = (DEPTH + 2) // NUM_MIXERS
N_FOX_LAYERS = (DEPTH + 1) // NUM_MIXERS
N_DIL_LAYERS = DEPTH // NUM_MIXERS

kernel_name = 'hybrid_interleaved_ssd_mla_fox_dilated'


def rms_norm(x, w):
    xf = x.astype(jnp.float32)
    return (xf * lax.rsqrt(jnp.mean(xf * xf, axis=-1, keepdims=True) + RMS_EPS)).astype(x.dtype) * w


def rope(x, positions):
    half = x.shape[-1] // 2
    inv_freq = ROPE_THETA ** (-jnp.arange(half, dtype=jnp.float32) / half)
    ang = positions.astype(jnp.float32)[:, None] * inv_freq[None, :]
    cos = jnp.cos(ang)[None, :, None, :]
    sin = jnp.sin(ang)[None, :, None, :]
    x1 = x[..., :half].astype(jnp.float32)
    x2 = x[..., half:].astype(jnp.float32)
    return jnp.concatenate([x1 * cos - x2 * sin, x2 * cos + x1 * sin], axis=-1).astype(x.dtype)


def partial_rope(x, positions):
    return jnp.concatenate([rope(x[..., :DIL_ROPE_DIM], positions), x[..., DIL_ROPE_DIM:]], axis=-1)


def causal_depthwise_conv(x, w, b):
    k = w.shape[0]
    y = lax.conv_general_dilated(x, w[:, None, :], window_strides=(1,), padding=[(k - 1, 0)],
                                 dimension_numbers=('NWC', 'WIO', 'NWC'), feature_group_count=x.shape[-1])
    return y + b


def gated_group_rms_norm(y, z, w, groups):
    g = y * jax.nn.silu(z.astype(jnp.float32))
    shp = g.shape
    gg = g.reshape(shp[:-1] + (groups, shp[-1] // groups))
    gg = gg * lax.rsqrt(jnp.mean(gg * gg, axis=-1, keepdims=True) + RMS_EPS)
    return gg.reshape(shp) * w


def ssd_chunked(xdt, a, b_in, c_out):
    bsz, seq, heads, p = xdt.shape
    g, n = b_in.shape[2], b_in.shape[3]
    r = heads // g
    t = SSM_CHUNK
    nc = seq // t
    xc = xdt.reshape(bsz, nc, t, g, r, p)
    bc = b_in.reshape(bsz, nc, t, g, n)
    cc = c_out.reshape(bsz, nc, t, g, n)
    a_cum = jnp.cumsum(a.reshape(bsz, nc, t, g, r).transpose(0, 3, 4, 1, 2), axis=-1)
    causal = jnp.tril(jnp.ones((t, t), dtype=bool))
    seg = a_cum[..., :, None] - a_cum[..., None, :]
    decay_in = jnp.exp(jnp.where(causal, seg, -jnp.inf))
    cb = jnp.einsum('bctgn,bcsgn->bgcts', cc, bc)
    y_diag = jnp.einsum('bgcts,bgrcts,bcsgrp->bctgrp', cb, decay_in, xc)
    decay_to_end = jnp.exp(a_cum[..., -1:] - a_cum)
    chunk_states = jnp.einsum('bctgn,bgrct,bctgrp->bcgrpn', bc, decay_to_end, xc)
    chunk_decay = jnp.exp(a_cum[..., -1])

    def carry_state(state, inp):
        st, dec = inp
        return state * dec[..., None, None] + st, state

    init = jnp.zeros_like(chunk_states[:, 0])
    _, entering = lax.scan(carry_state, init,
                           (chunk_states.swapaxes(0, 1), chunk_decay.transpose(3, 0, 1, 2)))
    entering = entering.swapaxes(0, 1)
    y_off = jnp.einsum('bctgn,bcgrpn,bgrct->bctgrp', cc, entering, jnp.exp(a_cum))
    return (y_diag + y_off).reshape(bsz, seq, heads, p)


def causal_block_attention(q, k, v, scale, log_forget_cum=None):
    bsz, seq, heads, dk = q.shape
    nb = seq // ATTN_BLOCK
    key_pos = jnp.arange(seq)
    q_blocks = q.reshape(bsz, nb, ATTN_BLOCK, heads, dk).swapaxes(0, 1)
    block_ids = jnp.arange(nb)
    use_decay = log_forget_cum is not None
    if use_decay:
        c_k = log_forget_cum.astype(jnp.float32).transpose(0, 2, 1)
        c_blocks = c_k.reshape(bsz, heads, nb, ATTN_BLOCK).transpose(2, 0, 1, 3)
        xs = (q_blocks, block_ids, c_blocks)
    else:
        xs = (q_blocks, block_ids)

    def attend(args):
        qb, bi = args[0], args[1]
        s = jnp.einsum('bqhd,bkhd->bhqk', qb, k).astype(jnp.float32) * scale
        if use_decay:
            s = s + args[2][..., :, None] - c_k[:, :, None, :]
        q_pos = bi * ATTN_BLOCK + jnp.arange(ATTN_BLOCK)
        s = jnp.where(key_pos[None, :] <= q_pos[:, None], s, -jnp.inf)
        p = jax.nn.softmax(s, axis=-1)
        return jnp.einsum('bhqk,bkhd->bqhd', p.astype(v.dtype), v)

    out = lax.map(attend, xs)
    return out.swapaxes(0, 1).reshape(bsz, seq, heads, v.shape[-1])


def dilated_band_attention(q, k, v, dilation, span):
    bsz, seq, heads, dh = q.shape
    length = seq // dilation
    nseq = bsz * dilation

    def to_residue(u):
        return u.reshape(bsz, length, dilation, heads, dh).transpose(0, 2, 1, 3, 4).reshape(nseq, length, heads, dh)

    blk = min(span, length)
    nb = -(-length // blk)
    padded = nb * blk
    padw = ((0, 0), (0, padded - length), (0, 0), (0, 0))
    qs = jnp.pad(to_residue(q), padw).reshape(nseq, nb, blk, heads, dh)
    ks = jnp.pad(to_residue(k), padw).reshape(nseq, nb, blk, heads, dh)
    vs = jnp.pad(to_residue(v), padw).reshape(nseq, nb, blk, heads, dh)

    def with_prev(u):
        prev = jnp.pad(u, ((0, 0), (1, 0), (0, 0), (0, 0), (0, 0)))[:, :nb]
        return jnp.concatenate([prev, u], axis=2)

    kw, vw = with_prev(ks), with_prev(vs)
    s = jnp.einsum('nbqhd,nbkhd->nbhqk', qs, kw).astype(jnp.float32) * (dh ** -0.5)
    q_idx = jnp.arange(blk)[:, None] + blk
    k_idx = jnp.arange(2 * blk)[None, :]
    dist = q_idx - k_idx
    key_abs = (jnp.arange(nb) * blk - blk)[:, None] + jnp.arange(2 * blk)[None, :]
    valid = ((dist >= 0) & (dist <= span))[None, :, :] & (key_abs >= 0)[:, None, :]
    s = jnp.where(valid[None, :, None, :, :], s, -jnp.inf)
    m = jnp.max(s, axis=-1, keepdims=True)
    p = jnp.exp(s - m)
    den = jnp.sum(p, axis=-1)
    o = jnp.einsum('nbhqk,nbkhd->nbqhd', p, vw.astype(jnp.float32)) / den.transpose(0, 1, 3, 2)[..., None]
    lse = (m[..., 0] + jnp.log(den)).transpose(0, 1, 3, 2)
    o = o.reshape(nseq, padded, heads, dh)[:, :length]
    lse = lse.reshape(nseq, padded, heads)[:, :length]
    o = o.reshape(bsz, dilation, length, heads, dh).transpose(0, 2, 1, 3, 4).reshape(bsz, seq, heads, dh)
    lse = lse.reshape(bsz, dilation, length, heads).transpose(0, 2, 1, 3).reshape(bsz, seq, heads)
    return o, lse


def ssd_mixer(u, in_w, conv_w, conv_b, dt_bias, a_log, d_skip, norm_w, out_w):
    bsz, seq, _ = u.shape
    proj = u @ in_w
    z = proj[..., :SSM_D_INNER]
    xbc = proj[..., SSM_D_INNER:SSM_D_INNER + SSM_CONV_DIM]
    dt_raw = proj[..., SSM_D_INNER + SSM_CONV_DIM:]
    xbc = jax.nn.silu(causal_depthwise_conv(xbc, conv_w, conv_b)).astype(jnp.float32)
    xs = xbc[..., :SSM_D_INNER].reshape(bsz, seq, SSM_N_HEADS, SSM_HEAD_DIM)
    b_in = xbc[..., SSM_D_INNER:SSM_D_INNER + SSM_BC_DIM].reshape(bsz, seq, SSM_N_GROUPS, SSM_D_STATE)
    c_out = xbc[..., SSM_D_INNER + SSM_BC_DIM:].reshape(bsz, seq, SSM_N_GROUPS, SSM_D_STATE)
    dt = jax.nn.softplus(dt_raw.astype(jnp.float32) + dt_bias.astype(jnp.float32))
    a = dt * (-jnp.exp(a_log.astype(jnp.float32)))
    y = ssd_chunked(xs * dt[..., None], a, b_in, c_out)
    y = y + d_skip.astype(jnp.float32)[:, None] * xs
    y = gated_group_rms_norm(y.reshape(bsz, seq, SSM_D_INNER), z, norm_w, SSM_N_GROUPS)
    return y.astype(u.dtype) @ out_w


def mla_mixer(u, positions, in_w, q_norm_w, kv_norm_w, uq_w, ukv_w, out_w):
    bsz, seq, _ = u.shape
    proj = u @ in_w
    o1 = MLA_Q_LORA
    o2 = o1 + MLA_KV_LORA
    o3 = o2 + MLA_ROPE_DIM
    c_q, c_kv, k_r, z = proj[..., :o1], proj[..., o1:o2], proj[..., o2:o3], proj[..., o3:]
    q = (rms_norm(c_q, q_norm_w) @ uq_w).reshape(bsz, seq, MLA_N_HEADS, MLA_QK_DIM)
    kv = (rms_norm(c_kv, kv_norm_w) @ ukv_w).reshape(bsz, seq, MLA_N_HEADS, MLA_NOPE_DIM + MLA_V_DIM)
    q_nope, q_rope = q[..., :MLA_NOPE_DIM], rope(q[..., MLA_NOPE_DIM:], positions)
    k_nope, v = kv[..., :MLA_NOPE_DIM], kv[..., MLA_NOPE_DIM:]
    k_rope = rope(k_r[:, :, None, :], positions)
    q_full = jnp.concatenate([q_nope, q_rope], axis=-1)
    k_full = jnp.concatenate([k_nope, jnp.broadcast_to(k_rope, (bsz, seq, MLA_N_HEADS, MLA_ROPE_DIM))], axis=-1)
    o = causal_block_attention(q_full, k_full, v, MLA_QK_DIM ** -0.5)
    o = o.reshape(bsz, seq, MLA_WIDTH) * jax.nn.silu(z)
    return o @ out_w


def fox_mixer(u, in_w, f_bias, out_w):
    bsz, seq, _ = u.shape
    proj = u @ in_w
    w = FOX_WIDTH
    q = proj[..., :w].reshape(bsz, seq, FOX_N_HEADS, FOX_HEAD_DIM)
    k = proj[..., w:2 * w].reshape(bsz, seq, FOX_N_HEADS, FOX_HEAD_DIM)
    v = proj[..., 2 * w:3 * w].reshape(bsz, seq, FOX_N_HEADS, FOX_HEAD_DIM)
    f_logit = proj[..., 3 * w:3 * w + FOX_N_HEADS]
    z = proj[..., 3 * w + FOX_N_HEADS:]
    log_f = jax.nn.log_sigmoid(f_logit.astype(jnp.float32) + f_bias.astype(jnp.float32))
    c = jnp.cumsum(log_f, axis=1)
    o = causal_block_attention(q, k, v, FOX_HEAD_DIM ** -0.5, log_forget_cum=c)
    o = o.reshape(bsz, seq, FOX_WIDTH) * jax.nn.silu(z)
    return o @ out_w


def dilated_mixer(u, positions, in_w, out_w):
    bsz, seq, _ = u.shape
    proj = u @ in_w
    w = DIL_WIDTH
    outs, lses = [], []
    for gi, (window, dilation) in enumerate(DIL_CONFIGS):
        base = 3 * w * gi
        q = proj[..., base:base + w].reshape(bsz, seq, DIL_N_HEADS, DIL_HEAD_DIM)
        k = proj[..., base + w:base + 2 * w].reshape(bsz, seq, DIL_N_HEADS, DIL_HEAD_DIM)
        v = proj[..., base + 2 * w:base + 3 * w].reshape(bsz, seq, DIL_N_HEADS, DIL_HEAD_DIM)
        o, lse = dilated_band_attention(partial_rope(q, positions), partial_rope(k, positions), v,
                                        dilation, window // dilation)
        outs.append(o)
        lses.append(lse)
    mix = jax.nn.softmax(jnp.stack(lses, axis=0), axis=0)
    o = jnp.sum(mix[..., None] * jnp.stack(outs, axis=0), axis=0)
    z = proj[..., 3 * w * len(DIL_CONFIGS):]
    o = o.reshape(bsz, seq, w).astype(u.dtype) * jax.nn.silu(z)
    return o @ out_w


def setup_inputs(seed: int = 0) -> dict:
    key = jax.random.key(seed)
    ks = iter(jax.random.split(key, 32))
    f32 = jnp.float32

    def dense(shape, fan_in):
        return jax.random.normal(next(ks), shape, f32) * (fan_in ** -0.5)

    def gain(shape):
        return 1.0 + 0.02 * jax.random.normal(next(ks), shape, f32)

    x = jax.random.normal(next(ks), (BATCH, SEQ, D_MODEL), f32)
    norm_w = gain((DEPTH, D_MODEL))
    final_norm_w = gain((D_MODEL,))
    ssm_in_w = dense((N_SSM_LAYERS, D_MODEL, SSM_IN_COLS), D_MODEL)
    ssm_conv_w = dense((N_SSM_LAYERS, SSM_CONV, SSM_CONV_DIM), SSM_CONV)
    ssm_conv_b = 0.02 * jax.random.normal(next(ks), (N_SSM_LAYERS, SSM_CONV_DIM), f32)
    dt0 = jnp.exp(jax.random.uniform(next(ks), (N_SSM_LAYERS, SSM_N_HEADS), f32,
                                     minval=np.log(SSM_DT_MIN), maxval=np.log(SSM_DT_MAX)))
    ssm_dt_bias = dt0 + jnp.log(-jnp.expm1(-dt0))
    ssm_A_log = jnp.log(jax.random.uniform(next(ks), (N_SSM_LAYERS, SSM_N_HEADS), f32, minval=1.0, maxval=16.0))
    ssm_D = gain((N_SSM_LAYERS, SSM_N_HEADS))
    ssm_norm_w = gain((N_SSM_LAYERS, SSM_D_INNER))
    ssm_out_w = dense((N_SSM_LAYERS, SSM_D_INNER, D_MODEL), SSM_D_INNER)
    mla_in_w = dense((N_MLA_LAYERS, D_MODEL, MLA_IN_COLS), D_MODEL)
    mla_q_norm_w = gain((N_MLA_LAYERS, MLA_Q_LORA))
    mla_kv_norm_w = gain((N_MLA_LAYERS, MLA_KV_LORA))
    mla_uq_w = dense((N_MLA_LAYERS, MLA_Q_LORA, MLA_N_HEADS * MLA_QK_DIM), MLA_Q_LORA)
    mla_ukv_w = dense((N_MLA_LAYERS, MLA_KV_LORA, MLA_N_HEADS * (MLA_NOPE_DIM + MLA_V_DIM)), MLA_KV_LORA)
    mla_out_w = dense((N_MLA_LAYERS, MLA_WIDTH, D_MODEL), MLA_WIDTH)
    fox_in_w = dense((N_FOX_LAYERS, D_MODEL, FOX_IN_COLS), D_MODEL)
    fox_f_bias = FOX_F_BIAS_MEAN + 0.5 * jax.random.normal(next(ks), (N_FOX_LAYERS, FOX_N_HEADS), f32)
    fox_out_w = dense((N_FOX_LAYERS, FOX_WIDTH, D_MODEL), FOX_WIDTH)
    dil_in_w = dense((N_DIL_LAYERS, D_MODEL, DIL_IN_COLS), D_MODEL)
    dil_out_w = dense((N_DIL_LAYERS, DIL_WIDTH, D_MODEL), DIL_WIDTH)
    return {'x': x, 'norm_w': norm_w, 'final_norm_w': final_norm_w,
            'ssm_in_w': ssm_in_w, 'ssm_conv_w': ssm_conv_w, 'ssm_conv_b': ssm_conv_b,
            'ssm_dt_bias': ssm_dt_bias, 'ssm_A_log': ssm_A_log, 'ssm_D': ssm_D,
            'ssm_norm_w': ssm_norm_w, 'ssm_out_w': ssm_out_w,
            'mla_in_w': mla_in_w, 'mla_q_norm_w': mla_q_norm_w, 'mla_kv_norm_w': mla_kv_norm_w,
            'mla_uq_w': mla_uq_w, 'mla_ukv_w': mla_ukv_w, 'mla_out_w': mla_out_w,
            'fox_in_w': fox_in_w, 'fox_f_bias': fox_f_bias, 'fox_out_w': fox_out_w,
            'dil_in_w': dil_in_w, 'dil_out_w': dil_out_w}


def reference(x, norm_w, final_norm_w, ssm_in_w, ssm_conv_w, ssm_conv_b, ssm_dt_bias, ssm_A_log, ssm_D,
              ssm_norm_w, ssm_out_w, mla_in_w, mla_q_norm_w, mla_kv_norm_w, mla_uq_w, mla_ukv_w, mla_out_w,
              fox_in_w, fox_f_bias, fox_out_w, dil_in_w, dil_out_w):
    positions = jnp.arange(x.shape[1])
    h = x
    for i in range(DEPTH):
        kind, j = i % NUM_MIXERS, i // NUM_MIXERS
        u = rms_norm(h, norm_w[i])
        if kind == 0:
            y = ssd_mixer(u, ssm_in_w[j], ssm_conv_w[j], ssm_conv_b[j], ssm_dt_bias[j], ssm_A_log[j],
                          ssm_D[j], ssm_norm_w[j], ssm_out_w[j])
        elif kind == 1:
            y = mla_mixer(u, positions, mla_in_w[j], mla_q_norm_w[j], mla_kv_norm_w[j], mla_uq_w[j],
                          mla_ukv_w[j], mla_out_w[j])
        elif kind == 2:
            y = fox_mixer(u, fox_in_w[j], fox_f_bias[j], fox_out_w[j])
        else:
            y = dilated_mixer(u, positions, dil_in_w[j], dil_out_w[j])
        h = h + y.astype(h.dtype)
    return rms_norm(h, final_norm_w)
```

```python
import functools
import math

import jax
import jax.numpy as jnp
from jax import lax
from jax.experimental import pallas as pl
from jax.experimental.pallas import tpu as pltpu

F32 = jnp.float32
BF16 = jnp.bfloat16

D_MODEL = 1024
RMS_EPS = 1e-6
ROPE_THETA = 500000.0
LOG2E = 1.4426950408889634
NEG_BIG = -1e30

LANES = 128

SSM_D_INNER = 2048
SSM_HEAD_DIM = 64
SSM_N_HEADS = 32
SSM_D_STATE = 128
SSM_N_GROUPS = 8
SSM_HEADS_PER_GROUP = SSM_N_HEADS // SSM_N_GROUPS
SSM_GROUP_WIDTH = SSM_HEADS_PER_GROUP * SSM_HEAD_DIM
SSM_CHUNK = 128
SSM_CONV = 4
SSM_BC_DIM = SSM_N_GROUPS * SSM_D_STATE
SSM_MAIN_COLS = 2 * SSM_D_INNER + 2 * SSM_BC_DIM
CONV_HALO = 16

MLA_N_HEADS = 16
MLA_Q_LORA = 384
MLA_KV_LORA = 256
MLA_NOPE_DIM = 128
MLA_ROPE_DIM = 64
MLA_V_DIM = 128
MLA_QK_DIM = MLA_NOPE_DIM + MLA_ROPE_DIM
MLA_WIDTH = MLA_N_HEADS * MLA_V_DIM
MLA_Q_PAD = 2 * LANES
MLA_SMALL_COLS = 768

FOX_N_HEADS = 16
FOX_HEAD_DIM = 128
FOX_WIDTH = FOX_N_HEADS * FOX_HEAD_DIM

DIL_CONFIGS = ((128, 1), (512, 4), (2048, 16))
DIL_N_HEADS = 8
DIL_HEAD_DIM = 128
DIL_WIDTH = DIL_N_HEADS * DIL_HEAD_DIM
DIL_ROPE_DIM = DIL_HEAD_DIM // 4

VMEM_LIMIT_BYTES = 56 * 1024 * 1024


def _params(n_axes):
    return pltpu.CompilerParams(dimension_semantics=("arbitrary",) * n_axes,
                                vmem_limit_bytes=VMEM_LIMIT_BYTES)


def _silu(x):
    return x * jax.nn.sigmoid(x)


def _rms_normalize(x, w):
    ms = jnp.mean(x * x, axis=-1, keepdims=True)
    return (x * lax.rsqrt(ms + RMS_EPS)) * w


def _rope_block(x, cos_t, sin_lo, sin_hi, half):
    return (x * cos_t + pltpu.roll(x, LANES - half, axis=1) * sin_lo
            + pltpu.roll(x, half, axis=1) * sin_hi)


def _rope_tables(seq, rope_dim):
    half = rope_dim // 2
    inv_freq = ROPE_THETA ** (-jnp.arange(half, dtype=F32) / half)
    ang = jnp.arange(seq, dtype=F32)[:, None] * inv_freq[None, :]
    cos, sin = jnp.cos(ang), jnp.sin(ang)
    pad = LANES - rope_dim
    cos_t = jnp.concatenate([cos, cos, jnp.ones((seq, pad), F32)], axis=1)
    zeros_h = jnp.zeros((seq, half), F32)
    zeros_p = jnp.zeros((seq, pad), F32)
    sin_lo = jnp.concatenate([-sin, zeros_h, zeros_p], axis=1)
    sin_hi = jnp.concatenate([zeros_h, sin, zeros_p], axis=1)
    return cos_t, sin_lo, sin_hi


def _rmsnorm_kernel(x_ref, w_ref, o_ref):
    o_ref[...] = _rms_normalize(x_ref[...], w_ref[...]).astype(o_ref.dtype)


def _rmsnorm(x, w, out_dtype, tm=1024):
    m, d = x.shape
    tm = min(tm, m)
    return pl.pallas_call(
        _rmsnorm_kernel,
        grid=(m // tm,),
        in_specs=[pl.BlockSpec((tm, d), lambda i: (i, 0)), pl.BlockSpec((1, d), lambda i: (0, 0))],
        out_specs=pl.BlockSpec((tm, d), lambda i: (i, 0)),
        out_shape=jax.ShapeDtypeStruct((m, d), out_dtype),
        compiler_params=_params(1),
        name="rmsnorm",
    )(x, w.reshape(1, d))


def _mm_kernel(a_ref, w_ref, o_ref):
    o_ref[...] = jnp.dot(a_ref[...], w_ref[...], preferred_element_type=F32).astype(o_ref.dtype)


def _mm(a, w, out_dtype, tm, tn, name):
    m, k = a.shape
    n = w.shape[1]
    tm = min(tm, m)
    return pl.pallas_call(
        _mm_kernel,
        grid=(m // tm, n // tn),
        in_specs=[pl.BlockSpec((tm, k), lambda i, j: (i, 0)), pl.BlockSpec((k, tn), lambda i, j: (0, j))],
        out_specs=pl.BlockSpec((tm, tn), lambda i, j: (i, j)),
        out_shape=jax.ShapeDtypeStruct((m, n), out_dtype),
        compiler_params=_params(2),
        name=name,
    )(a, w)


def _mm_rope_kernel(a_ref, w_ref, cos_ref, slo_ref, shi_ref, o_ref, *, half):
    acc = jnp.dot(a_ref[...], w_ref[...], preferred_element_type=F32)
    cos_t, sin_lo, sin_hi = cos_ref[...], slo_ref[...], shi_ref[...]
    for blk in range(acc.shape[1] // LANES):
        cols = slice(blk * LANES, (blk + 1) * LANES)
        o_ref[:, cols] = _rope_block(acc[:, cols], cos_t, sin_lo, sin_hi, half).astype(o_ref.dtype)


def _mm_rope(a, w, tables, seq, rope_dim, out_dtype, tm, tn, name):
    m, k = a.shape
    n = w.shape[1]
    tm = min(tm, seq)
    row_blocks = seq // tm
    tab_spec = pl.BlockSpec((tm, LANES), lambda i, j: (i % row_blocks, 0))
    return pl.pallas_call(
        functools.partial(_mm_rope_kernel, half=rope_dim // 2),
        grid=(m // tm, n // tn),
        in_specs=[pl.BlockSpec((tm, k), lambda i, j: (i, 0)), pl.BlockSpec((k, tn), lambda i, j: (0, j)),
                  tab_spec, tab_spec, tab_spec],
        out_specs=pl.BlockSpec((tm, tn), lambda i, j: (i, j)),
        out_shape=jax.ShapeDtypeStruct((m, n), out_dtype),
        compiler_params=_params(2),
        name=name,
    )(a, w, *tables)


def _mla_small_kernel(a_ref, w_ref, cos_ref, slo_ref, shi_ref, o_ref, kr_ref):
    acc = jnp.dot(a_ref[...], w_ref[...], preferred_element_type=F32)
    o_ref[...] = acc
    k_r = acc[:, MLA_Q_LORA:MLA_Q_LORA + LANES]
    kr_ref[...] = _rope_block(k_r, cos_ref[...], slo_ref[...], shi_ref[...], MLA_ROPE_DIM // 2).astype(kr_ref.dtype)


def _mla_small(u, w, tables, seq, tm=1024):
    m, k = u.shape
    tm = min(tm, seq)
    row_blocks = seq // tm
    tab_spec = pl.BlockSpec((tm, LANES), lambda i: (i % row_blocks, 0))
    return pl.pallas_call(
        _mla_small_kernel,
        grid=(m // tm,),
        in_specs=[pl.BlockSpec((tm, k), lambda i: (i, 0)), pl.BlockSpec((k, MLA_SMALL_COLS), lambda i: (0, 0)),
                  tab_spec, tab_spec, tab_spec],
        out_specs=[pl.BlockSpec((tm, MLA_SMALL_COLS), lambda i: (i, 0)),
                   pl.BlockSpec((tm, LANES), lambda i: (i, 0))],
        out_shape=[jax.ShapeDtypeStruct((m, MLA_SMALL_COLS), F32), jax.ShapeDtypeStruct((m, LANES), BF16)],
        compiler_params=_params(1),
        name="mla_in_small",
    )(u, w, *tables)


def _norm_mm_kernel(a_ref, nw_ref, w_ref, *rest, rope_half):
    if rope_half:
        cos_ref, slo_ref, shi_ref, o_ref, an_ref = rest
    else:
        o_ref, an_ref = rest

    @pl.when(pl.program_id(1) == 0)
    def _():
        an_ref[...] = _rms_normalize(a_ref[...], nw_ref[...]).astype(an_ref.dtype)

    acc = jnp.dot(an_ref[...], w_ref[...], preferred_element_type=F32)
    if not rope_half:
        o_ref[...] = acc.astype(o_ref.dtype)
        return
    cos_t, sin_lo, sin_hi = cos_ref[...], slo_ref[...], shi_ref[...]
    for blk in range(acc.shape[1] // LANES):
        cols = slice(blk * LANES, (blk + 1) * LANES)
        x = acc[:, cols]
        if blk % 2 == 1:
            x = _rope_block(x, cos_t, sin_lo, sin_hi, rope_half)
        o_ref[:, cols] = x.astype(o_ref.dtype)


def _norm_mm(a, a_col_block, k, norm_w, w, out_dtype, tm, tn, name, rope=None, seq=None):
    m = a.shape[0]
    n = w.shape[1]
    tm = min(tm, m if seq is None else seq)
    in_specs = [pl.BlockSpec((tm, k), lambda i, j: (i, a_col_block)),
                pl.BlockSpec((1, k), lambda i, j: (0, 0)),
                pl.BlockSpec((k, tn), lambda i, j: (0, j))]
    args = [a, norm_w.reshape(1, k), w]
    rope_half = 0
    if rope is not None:
        tables, rope_dim = rope
        rope_half = rope_dim // 2
        row_blocks = seq // tm
        in_specs += [pl.BlockSpec((tm, LANES), lambda i, j: (i % row_blocks, 0))] * 3
        args += list(tables)
    return pl.pallas_call(
        functools.partial(_norm_mm_kernel, rope_half=rope_half),
        grid=(m // tm, n // tn),
        in_specs=in_specs,
        out_specs=pl.BlockSpec((tm, tn), lambda i, j: (i, j)),
        out_shape=jax.ShapeDtypeStruct((m, n), out_dtype),
        scratch_shapes=[pltpu.VMEM((tm, k), BF16)],
        compiler_params=_params(2),
        name=name,
    )(*args)


def _out_proj_kernel(a_ref, w_ref, h_ref, nw_ref, *out_refs, final):
    h_new = h_ref[...] + jnp.dot(a_ref[...], w_ref[...], preferred_element_type=F32)
    if final:
        (u_ref,) = out_refs
    else:
        hn_ref, u_ref = out_refs
        hn_ref[...] = h_new
    u_ref[...] = _rms_normalize(h_new, nw_ref[...]).astype(u_ref.dtype)


def _out_proj(a, w, h, norm_w, final, tm=512):
    m, k = a.shape
    d = w.shape[1]
    tm = min(tm, m)
    row_spec = pl.BlockSpec((tm, d), lambda i: (i, 0))
    if final:
        out_specs, out_shape = row_spec, jax.ShapeDtypeStruct((m, d), F32)
    else:
        out_specs = [row_spec, row_spec]
        out_shape = [jax.ShapeDtypeStruct((m, d), F32), jax.ShapeDtypeStruct((m, d), BF16)]
    return pl.pallas_call(
        functools.partial(_out_proj_kernel, final=final),
        grid=(m // tm,),
        in_specs=[pl.BlockSpec((tm, k), lambda i: (i, 0)), pl.BlockSpec((k, d), lambda i: (0, 0)),
                  row_spec, pl.BlockSpec((1, d), lambda i: (0, 0))],
        out_specs=out_specs,
        out_shape=out_shape,
        compiler_params=_params(1),
        name="out_proj_final" if final else "out_proj",
    )(a, w, h, norm_w.reshape(1, d))


def _expand_heads(v):
    t = v.shape[0]
    cols = [jnp.broadcast_to(v[:, h:h + 1], (t, LANES)) for h in range(SSM_HEADS_PER_GROUP)]
    lane = lax.broadcasted_iota(jnp.int32, (t, LANES), 1)
    first = lane < SSM_HEAD_DIM
    return jnp.concatenate([jnp.where(first, cols[0], cols[1]), jnp.where(first, cols[2], cols[3])], axis=1)


def _cumsum_rows(x):
    n = x.shape[0]
    row = lax.broadcasted_iota(jnp.int32, x.shape, 0)
    shift = 1
    while shift < n:
        x = x + jnp.where(row >= shift, pltpu.roll(x, shift, axis=0), 0.0)
        shift *= 2
    return x


def _ssd_kernel(z_ref, x_ref, b_ref, c_ref, dt_ref, wx_ref, wb_ref, wc_ref, bx_ref, bb_ref, bc_ref,
                dtb_ref, alog_ref, dskip_ref, nw_ref, o_ref, *, seq):
    t = SSM_CHUNK
    n_chunks = seq // t

    def conv_silu(ref, w_ref, bias_ref, t0, c):
        cur = ref[pl.ds(t0, t), :].astype(F32)
        prev0 = pl.multiple_of(jnp.maximum(t0 - CONV_HALO, 0), CONV_HALO)
        prev = ref[pl.ds(prev0, CONV_HALO), :].astype(F32)
        prev = jnp.where(c > 0, prev, 0.0)
        ext = jnp.concatenate([prev, cur], axis=0)
        w = w_ref[...]
        acc = bias_ref[...] + w[SSM_CONV - 1:SSM_CONV, :] * cur
        for tap in range(SSM_CONV - 1):
            shifted = pltpu.roll(ext, SSM_CONV - 1 - tap, axis=0)[CONV_HALO:, :]
            acc = acc + w[tap:tap + 1, :] * shifted
        return _silu(acc)

    row = lax.broadcasted_iota(jnp.int32, (t, t), 0)
    col = lax.broadcasted_iota(jnp.int32, (t, t), 1)
    causal = row >= col
    head_of_lane = lax.broadcasted_iota(jnp.int32, (t, SSM_GROUP_WIDTH), 1) // SSM_HEAD_DIM
    neg_a = -jnp.exp(alog_ref[...])

    def chunk(c, state):
        t0 = pl.multiple_of(c * t, t)
        x = conv_silu(x_ref, wx_ref, bx_ref, t0, c)
        b_in = conv_silu(b_ref, wb_ref, bb_ref, t0, c)
        c_out = conv_silu(c_ref, wc_ref, bc_ref, t0, c)
        dt = jax.nn.softplus(dt_ref[pl.ds(t0, t), :] + dtb_ref[...])
        a_cum = _cumsum_rows(dt * neg_a)
        a_cum_t = a_cum.T
        a_last = a_cum[t - 1:t, :]
        exp_cum = _expand_heads(jnp.exp(a_cum))
        decay_to_end = _expand_heads(jnp.exp(a_last - a_cum))
        xdt = x * _expand_heads(dt)

        b_bf = b_in.astype(BF16)
        c_bf = c_out.astype(BF16)
        cb = lax.dot_general(c_bf, b_bf, (((1,), (1,)), ((), ())), preferred_element_type=F32)
        y = jnp.dot(c_bf, state.astype(BF16), preferred_element_type=F32) * exp_cum
        for h in range(SSM_HEADS_PER_GROUP):
            seg = jnp.broadcast_to(a_cum[:, h:h + 1], (t, t)) - jnp.broadcast_to(a_cum_t[h:h + 1, :], (t, t))
            decay = jnp.exp(jnp.where(causal, seg, NEG_BIG))
            x_h = jnp.where(head_of_lane == h, xdt, 0.0).astype(BF16)
            y = y + jnp.dot((cb * decay).astype(BF16), x_h, preferred_element_type=F32)
        new_state = state * exp_cum[t - 1:t, :] + jnp.dot(
            b_in.T.astype(BF16), (xdt * decay_to_end).astype(BF16), preferred_element_type=F32)

        y = y + dskip_ref[...] * x
        g = y * _silu(z_ref[pl.ds(t0, t), :].astype(F32))
        o_ref[pl.ds(t0, t), :] = _rms_normalize(g, nw_ref[...]).astype(o_ref.dtype)
        return new_state

    lax.fori_loop(0, n_chunks, chunk, jnp.zeros((SSM_D_STATE, SSM_GROUP_WIDTH), F32))


def _ssd_mixer(proj, dt_raw, conv_w, conv_b, dt_bias, a_log, d_skip, norm_w, batch, seq):
    m = proj.shape[0]
    gw, ns = SSM_GROUP_WIDTH, SSM_D_STATE
    x_blk0 = SSM_D_INNER // gw
    b_blk0 = 2 * SSM_D_INNER // ns
    c_blk0 = b_blk0 + SSM_N_GROUPS
    cw_b0 = SSM_D_INNER // ns
    cw_c0 = cw_b0 + SSM_N_GROUPS

    def pad_heads(v):
        out = jnp.zeros((SSM_N_GROUPS, 1, LANES), F32)
        return out.at[:, 0, :SSM_HEADS_PER_GROUP].set(v.astype(F32).reshape(SSM_N_GROUPS, SSM_HEADS_PER_GROUP))

    d_lanes = jnp.repeat(d_skip.astype(F32).reshape(SSM_N_GROUPS, SSM_HEADS_PER_GROUP), SSM_HEAD_DIM,
                         axis=1).reshape(SSM_N_GROUPS, 1, gw)
    conv_b2 = conv_b.reshape(1, -1)
    nw2 = norm_w.reshape(1, -1)
    seq_spec = lambda width, blk0: pl.BlockSpec((seq, width), lambda b, g: (b, blk0 + g))
    row_spec = lambda rows, width, blk0: pl.BlockSpec((rows, width), lambda b, g: (0, blk0 + g))
    grp_spec = lambda width: pl.BlockSpec((None, 1, width), lambda b, g: (g, 0, 0))
    return pl.pallas_call(
        functools.partial(_ssd_kernel, seq=seq),
        grid=(batch, SSM_N_GROUPS),
        in_specs=[seq_spec(gw, 0), seq_spec(gw, x_blk0), seq_spec(ns, b_blk0), seq_spec(ns, c_blk0),
                  seq_spec(LANES, 0),
                  row_spec(SSM_CONV, gw, 0), row_spec(SSM_CONV, ns, cw_b0), row_spec(SSM_CONV, ns, cw_c0),
                  row_spec(1, gw, 0), row_spec(1, ns, cw_b0), row_spec(1, ns, cw_c0),
                  grp_spec(LANES), grp_spec(LANES), grp_spec(gw), row_spec(1, gw, 0)],
        out_specs=pl.BlockSpec((seq, gw), lambda b, g: (b, g)),
        out_shape=jax.ShapeDtypeStruct((m, SSM_D_INNER), BF16),
        compiler_params=_params(2),
        name="ssd_mixer",
    )(proj, proj, proj, proj, dt_raw, conv_w, conv_w, conv_w, conv_b2, conv_b2, conv_b2,
      pad_heads(dt_bias), pad_heads(a_log), d_lanes, nw2)


def _flash_kernel(*refs, tile, shared_key, forget, heads_per_cum_block):
    refs = list(refs)
    q_ref, k_ref = refs[:2]
    pos = 2
    kr_ref = ck_ref = None
    if shared_key:
        kr_ref = refs[pos]
        pos += 1
    v_ref = refs[pos]
    pos += 1
    if forget:
        ck_ref = refs[pos]
        pos += 1
    z_ref, o_ref = refs[pos], refs[pos + 1]

    qi = pl.program_id(2)
    q = q_ref[...]
    if forget:
        cum_row = pl.program_id(1) % heads_per_cum_block

    def scores(j):
        k0 = pl.multiple_of(j * tile, tile)
        k = k_ref[pl.ds(k0, tile), :]
        if shared_key:
            k = jnp.concatenate([k, kr_ref[pl.ds(k0, tile), :]], axis=1)
        s = lax.dot_general(q, k, (((1,), (1,)), ((), ())), preferred_element_type=F32)
        if forget:
            s = s - ck_ref[pl.ds(cum_row, 1), pl.ds(k0, tile)]
        return s, v_ref[pl.ds(k0, tile), :]

    def update(carry, s, v):
        m, l, acc = carry
        m_new = jnp.maximum(m, jnp.max(s, axis=1, keepdims=True))
        alpha = jnp.exp2(m - m_new)
        p = jnp.exp2(s - m_new)
        l = alpha * l + jnp.sum(p, axis=1, keepdims=True)
        acc = alpha * acc + jnp.dot(p.astype(BF16), v, preferred_element_type=F32)
        return m_new, l, acc

    def below_diagonal(j, carry):
        s, v = scores(j)
        return update(carry, s, v)

    dv = v_ref.shape[1]
    init = (jnp.full((tile, 1), NEG_BIG, F32), jnp.zeros((tile, 1), F32), jnp.zeros((tile, dv), F32))
    carry = lax.fori_loop(0, qi, below_diagonal, init)
    s, v = scores(qi)
    row = lax.broadcasted_iota(jnp.int32, (tile, tile), 0)
    col = lax.broadcasted_iota(jnp.int32, (tile, tile), 1)
    _, l, acc = update(carry, jnp.where(row >= col, s, NEG_BIG), v)
    o = acc / l
    o_ref[...] = (o * _silu(z_ref[...].astype(F32))).astype(o_ref.dtype)


def _flash(q_arr, q_width, q_blk0, k_arr, k_blk0, v_arr, v_blk0, z_arr, z_blk0, batch, seq, heads,
           tile, name, kr_arr=None, cum_t=None):
    m = q_arr.shape[0]
    nq = seq // tile
    dv = LANES
    in_specs = [pl.BlockSpec((tile, q_width), lambda b, h, i: (b * nq + i, q_blk0 + h)),
                pl.BlockSpec((seq, LANES), lambda b, h, i: (b, k_blk0 + h))]
    args = [q_arr, k_arr]
    if kr_arr is not None:
        in_specs.append(pl.BlockSpec((seq, LANES), lambda b, h, i: (b, 0)))
        args.append(kr_arr)
    in_specs.append(pl.BlockSpec((seq, dv), lambda b, h, i: (b, v_blk0 + h)))
    args.append(v_arr)
    cum_rows = 8
    if cum_t is not None:
        in_specs.append(pl.BlockSpec((None, cum_rows, seq), lambda b, h, i: (b, h // cum_rows, 0)))
        args.append(cum_t)
    in_specs.append(pl.BlockSpec((tile, dv), lambda b, h, i: (b * nq + i, z_blk0 + h)))
    args.append(z_arr)
    return pl.pallas_call(
        functools.partial(_flash_kernel, tile=tile, shared_key=kr_arr is not None, forget=cum_t is not None,
                          heads_per_cum_block=cum_rows),
        grid=(batch, heads, nq),
        in_specs=in_specs,
        out_specs=pl.BlockSpec((tile, dv), lambda b, h, i: (b * nq + i, h)),
        out_shape=jax.ShapeDtypeStruct((m, heads * dv), BF16),
        compiler_params=_params(3),
        name=name,
    )(*args)


def _forget_cum_kernel(f_ref, b_ref, o_ref):
    x = f_ref[...] + b_ref[...]
    log_f = jnp.minimum(x, 0.0) - jnp.log1p(jnp.exp(-jnp.abs(x)))
    o_ref[...] = (_cumsum_rows(log_f) * LOG2E).T


def _forget_cum(f_raw, f_bias_pad, batch, seq):
    return pl.pallas_call(
        _forget_cum_kernel,
        grid=(batch,),
        in_specs=[pl.BlockSpec((seq, LANES), lambda b: (b, 0)), pl.BlockSpec((1, LANES), lambda b: (0, 0))],
        out_specs=pl.BlockSpec((None, LANES, seq), lambda b: (b, 0, 0)),
        out_shape=jax.ShapeDtypeStruct((batch, LANES, seq), F32),
        compiler_params=_params(1),
        name="forget_cum",
    )(f_raw, f_bias_pad)


def _dilated_kernel(*refs, seq):
    n_grp = len(DIL_CONFIGS)
    qkv_refs = refs[:3 * n_grp]
    z_ref, o_ref = refs[3 * n_grp], refs[3 * n_grp + 1]
    scratch = refs[3 * n_grp + 2:]
    q_s, k_s, v_s = scratch[:3]
    acc_s = scratch[3:3 + n_grp]
    max_s = scratch[3 + n_grp:3 + 2 * n_grp]
    den_s = scratch[3 + 2 * n_grp:3 + 3 * n_grp]

    for g, (window, dil) in enumerate(DIL_CONFIGS):
        q_ref, k_ref, v_ref = qkv_refs[3 * g:3 * g + 3]
        span = window // dil
        length = seq // dil
        blk = min(span, length)
        n_blk = length // blk
        q_s[...] = q_ref[...].astype(F32)
        k_s[...] = k_ref[...].astype(F32)
        v_s[...] = v_ref[...].astype(F32)
        row = lax.broadcasted_iota(jnp.int32, (blk, 2 * blk), 0)
        col = lax.broadcasted_iota(jnp.int32, (blk, 2 * blk), 1)
        dist = row + blk - col
        in_band = (dist >= 0) & (dist <= span)
        prev_half = col < blk

        def block(idx, carry, dil=dil, blk=blk, n_blk=n_blk, in_band=in_band, prev_half=prev_half,
                  acc_ref=acc_s[g], max_ref=max_s[g], den_ref=den_s[g]):
            residue = idx // n_blk
            n = idx - residue * n_blk
            start = residue + n * (blk * dil)
            prev = jnp.maximum(start - blk * dil, residue)
            rows = lambda s0: pl.ds(s0, blk, stride=dil) if dil > 1 else pl.ds(s0, blk)
            q = q_s[rows(start), :].astype(BF16)
            k = jnp.concatenate([k_s[rows(prev), :], k_s[rows(start), :]], axis=0).astype(BF16)
            v = jnp.concatenate([v_s[rows(prev), :], v_s[rows(start), :]], axis=0).astype(BF16)
            s = lax.dot_general(q, k, (((1,), (1,)), ((), ())), preferred_element_type=F32)
            no_prev = jnp.where(n > 0, 0.0, NEG_BIG)
            s = jnp.where(in_band, s, NEG_BIG)
            s = jnp.where(prev_half, s + no_prev, s)
            m = jnp.max(s, axis=1, keepdims=True)
            p = jnp.exp2(s - m)
            den = jnp.sum(p, axis=1, keepdims=True)
            acc_ref[rows(start), :] = jnp.dot(p.astype(BF16), v, preferred_element_type=F32)
            max_ref[rows(start), :] = jnp.broadcast_to(m, (blk, LANES))
            den_ref[rows(start), :] = jnp.broadcast_to(den, (blk, LANES))
            return carry

        lax.fori_loop(0, dil * n_blk, block, 0)

    rows_per_step = min(256, seq)

    def combine(i, carry):
        r0 = pl.multiple_of(i * rows_per_step, rows_per_step)
        rs = pl.ds(r0, rows_per_step)
        m_all = max_s[0][rs, :]
        for g in range(1, n_grp):
            m_all = jnp.maximum(m_all, max_s[g][rs, :])
        num = jnp.zeros((rows_per_step, LANES), F32)
        den = jnp.zeros((rows_per_step, LANES), F32)
        for g in range(n_grp):
            w = jnp.exp2(max_s[g][rs, :] - m_all)
            num = num + w * acc_s[g][rs, :]
            den = den + w * den_s[g][rs, :]
        o_ref[rs, :] = ((num / den) * _silu(z_ref[rs, :].astype(F32))).astype(o_ref.dtype)
        return carry

    lax.fori_loop(0, seq // rows_per_step, combine, 0)


def _dilated_mixer(qk, vz, batch, seq):
    m = qk.shape[0]
    n_grp = len(DIL_CONFIGS)
    hb = DIL_N_HEADS
    in_specs, args = [], []
    for g in range(n_grp):
        in_specs += [pl.BlockSpec((seq, LANES), lambda b, h, g=g: (b, 2 * hb * g + h)),
                     pl.BlockSpec((seq, LANES), lambda b, h, g=g: (b, 2 * hb * g + hb + h)),
                     pl.BlockSpec((seq, LANES), lambda b, h, g=g: (b, hb * g + h))]
        args += [qk, qk, vz]
    in_specs.append(pl.BlockSpec((seq, LANES), lambda b, h: (b, hb * n_grp + h)))
    args.append(vz)
    return pl.pallas_call(
        functools.partial(_dilated_kernel, seq=seq),
        grid=(batch, DIL_N_HEADS),
        in_specs=in_specs,
        out_specs=pl.BlockSpec((seq, LANES), lambda b, h: (b, h)),
        out_shape=jax.ShapeDtypeStruct((m, DIL_WIDTH), BF16),
        scratch_shapes=[pltpu.VMEM((seq, LANES), F32)] * (3 + 3 * n_grp),
        compiler_params=_params(2),
        name="dilated_mixer",
    )(*args)


def _ssd_layer(u, h, next_norm_w, final, in_w, conv_w, conv_b, dt_bias, a_log, d_skip, norm_w, out_w,
               batch, seq):
    w_main = in_w[:, :SSM_MAIN_COLS].astype(BF16)
    w_dt = in_w[:, SSM_MAIN_COLS:].reshape(D_MODEL, SSM_N_GROUPS, SSM_HEADS_PER_GROUP)
    w_dt = jnp.zeros((D_MODEL, SSM_N_GROUPS, LANES), F32).at[:, :, :SSM_HEADS_PER_GROUP].set(w_dt)
    w_dt = w_dt.reshape(D_MODEL, SSM_N_GROUPS * LANES).astype(BF16)
    proj = _mm(u, w_main, BF16, 1024, 512, "ssd_in_proj")
    dt_raw = _mm(u, w_dt, F32, 1024, 1024, "ssd_dt_proj")
    y = _ssd_mixer(proj, dt_raw, conv_w, conv_b, dt_bias, a_log, d_skip, norm_w, batch, seq)
    return _out_proj(y, out_w.astype(BF16), h, next_norm_w, final)


def _mla_layer(u, h, next_norm_w, final, in_w, q_norm_w, kv_norm_w, uq_w, ukv_w, out_w, batch, seq):
    o1 = MLA_Q_LORA
    o2 = o1 + MLA_KV_LORA
    o3 = o2 + MLA_ROPE_DIM
    w_small = jnp.concatenate([in_w[:, :o1], in_w[:, o2:o3], jnp.zeros((D_MODEL, LANES - MLA_ROPE_DIM), F32),
                               in_w[:, o1:o2]], axis=1).astype(BF16)
    w_z = in_w[:, o3:].astype(BF16)
    q_scale = MLA_QK_DIM ** -0.5 * LOG2E
    w_q = (uq_w * q_scale).reshape(MLA_Q_LORA, MLA_N_HEADS, MLA_QK_DIM)
    w_q = jnp.pad(w_q, ((0, 0), (0, 0), (0, MLA_Q_PAD - MLA_QK_DIM))).reshape(MLA_Q_LORA, -1).astype(BF16)
    w_kv = ukv_w.reshape(MLA_KV_LORA, MLA_N_HEADS, 2, MLA_NOPE_DIM).transpose(0, 2, 1, 3)
    w_kv = w_kv.reshape(MLA_KV_LORA, -1).astype(BF16)
    tables = _rope_tables(seq, MLA_ROPE_DIM)

    small, k_rope = _mla_small(u, w_small, tables, seq)
    z = _mm(u, w_z, BF16, 1024, 512, "mla_gate_proj")
    q = _norm_mm(small, 0, MLA_Q_LORA, q_norm_w, w_q, BF16, 1024, 512, "mla_q_proj",
                 rope=(tables, MLA_ROPE_DIM), seq=seq)
    kv = _norm_mm(small, (MLA_SMALL_COLS - MLA_KV_LORA) // MLA_KV_LORA, MLA_KV_LORA, kv_norm_w, w_kv, BF16,
                  1024, 512, "mla_kv_proj")
    o = _flash(q, MLA_Q_PAD, 0, kv, 0, kv, MLA_N_HEADS, z, 0, batch, seq, MLA_N_HEADS, 512, "mla_attention",
               kr_arr=k_rope)
    return _out_proj(o, out_w.astype(BF16), h, next_norm_w, final)


def _fox_layer(u, h, next_norm_w, final, in_w, f_bias, out_w, batch, seq):
    w = FOX_WIDTH
    q_scale = FOX_HEAD_DIM ** -0.5 * LOG2E
    w_main = jnp.concatenate([in_w[:, :w] * q_scale, in_w[:, w:3 * w], in_w[:, 3 * w + FOX_N_HEADS:]],
                             axis=1).astype(BF16)
    w_f = jnp.pad(in_w[:, 3 * w:3 * w + FOX_N_HEADS], ((0, 0), (0, LANES - FOX_N_HEADS))).astype(BF16)
    f_bias_pad = jnp.pad(f_bias.astype(F32), (0, LANES - FOX_N_HEADS)).reshape(1, LANES)
    proj = _mm(u, w_main, BF16, 1024, 512, "fox_in_proj")
    f_raw = _mm(u, w_f, F32, 1024, LANES, "fox_forget_proj")
    cum_t = _forget_cum(f_raw, f_bias_pad, batch, seq)
    hh = FOX_N_HEADS
    o = _flash(proj, LANES, 0, proj, hh, proj, 2 * hh, proj, 3 * hh, batch, seq, hh, 512, "fox_attention",
               cum_t=cum_t)
    return _out_proj(o, out_w.astype(BF16), h, next_norm_w, final)


def _dilated_layer(u, h, next_norm_w, final, in_w, out_w, batch, seq):
    w = DIL_WIDTH
    q_scale = DIL_HEAD_DIM ** -0.5 * LOG2E
    qk_cols, v_cols = [], []
    for gi in range(len(DIL_CONFIGS)):
        base = 3 * w * gi
        qk_cols += [in_w[:, base:base + w] * q_scale, in_w[:, base + w:base + 2 * w]]
        v_cols.append(in_w[:, base + 2 * w:base + 3 * w])
    w_qk = jnp.concatenate(qk_cols, axis=1).astype(BF16)
    w_vz = jnp.concatenate(v_cols + [in_w[:, 3 * w * len(DIL_CONFIGS):]], axis=1).astype(BF16)
    tables = _rope_tables(seq, DIL_ROPE_DIM)
    qk = _mm_rope(u, w_qk, tables, seq, DIL_ROPE_DIM, BF16, 1024, 512, "dilated_qk_proj")
    vz = _mm(u, w_vz, BF16, 1024, 512, "dilated_vz_proj")
    o = _dilated_mixer(qk, vz, batch, seq)
    return _out_proj(o, out_w.astype(BF16), h, next_norm_w, final)


def kernel(x, norm_w, final_norm_w, ssm_in_w, ssm_conv_w, ssm_conv_b, ssm_dt_bias, ssm_A_log, ssm_D,
           ssm_norm_w, ssm_out_w, mla_in_w, mla_q_norm_w, mla_kv_norm_w, mla_uq_w, mla_ukv_w, mla_out_w,
           fox_in_w, fox_f_bias, fox_out_w, dil_in_w, dil_out_w):
    batch, seq, d = x.shape
    depth = norm_w.shape[0]
    h = x.reshape(batch * seq, d)
    u = _rmsnorm(h, norm_w[0], BF16)
    for i in range(depth):
        kind, j = i % 4, i // 4
        final = i == depth - 1
        next_w = final_norm_w if final else norm_w[i + 1]
        if kind == 0:
            res = _ssd_layer(u, h, next_w, final, ssm_in_w[j], ssm_conv_w[j], ssm_conv_b[j], ssm_dt_bias[j],
                             ssm_A_log[j], ssm_D[j], ssm_norm_w[j], ssm_out_w[j], batch, seq)
        elif kind == 1:
            res = _mla_layer(u, h, next_w, final, mla_in_w[j], mla_q_norm_w[j], mla_kv_norm_w[j], mla_uq_w[j],
                             mla_ukv_w[j], mla_out_w[j], batch, seq)
        elif kind == 2:
            res = _fox_layer(u, h, next_w, final, fox_in_w[j], fox_f_bias[j], fox_out_w[j], batch, seq)
        else:
            res = _dilated_layer(u, h, next_w, final, dil_in_w[j], dil_out_w[j], batch, seq)
        if final:
            return res.reshape(batch, seq, d)
        h, u = res
```

```python
import functools
import math

import jax
import jax.numpy as jnp
from jax import lax
from jax.experimental import pallas as pl
from jax.experimental.pallas import tpu as pltpu

F32 = jnp.float32
BF16 = jnp.bfloat16

D_MODEL = 1024
RMS_EPS = 1e-6
ROPE_THETA = 500000.0
LOG2E = 1.4426950408889634
NEG_BIG = -1e30

LANES = 128

SSM_D_INNER = 2048
SSM_HEAD_DIM = 64
SSM_N_HEADS = 32
SSM_D_STATE = 128
SSM_N_GROUPS = 8
SSM_HEADS_PER_GROUP = SSM_N_HEADS // SSM_N_GROUPS
SSM_GROUP_WIDTH = SSM_HEADS_PER_GROUP * SSM_HEAD_DIM
SSM_CHUNK = 128
SSM_CONV = 4
SSM_BC_DIM = SSM_N_GROUPS * SSM_D_STATE
SSM_MAIN_COLS = 2 * SSM_D_INNER + 2 * SSM_BC_DIM
CONV_HALO = 16

MLA_N_HEADS = 16
MLA_Q_LORA = 384
MLA_KV_LORA = 256
MLA_NOPE_DIM = 128
MLA_ROPE_DIM = 64
MLA_V_DIM = 128
MLA_QK_DIM = MLA_NOPE_DIM + MLA_ROPE_DIM
MLA_WIDTH = MLA_N_HEADS * MLA_V_DIM
MLA_Q_PAD = 2 * LANES
MLA_SMALL_COLS = 768

FOX_N_HEADS = 16
FOX_HEAD_DIM = 128
FOX_WIDTH = FOX_N_HEADS * FOX_HEAD_DIM

DIL_CONFIGS = ((128, 1), (512, 4), (2048, 16))
DIL_N_HEADS = 8
DIL_HEAD_DIM = 128
DIL_WIDTH = DIL_N_HEADS * DIL_HEAD_DIM
DIL_ROPE_DIM = DIL_HEAD_DIM // 4

VMEM_LIMIT_BYTES = 56 * 1024 * 1024


def _params(n_axes):
    return pltpu.CompilerParams(dimension_semantics=("arbitrary",) * n_axes,
                                vmem_limit_bytes=VMEM_LIMIT_BYTES)


def _silu(x):
    return x * jax.nn.sigmoid(x)


def _rms_normalize(x, w):
    ms = jnp.mean(x * x, axis=-1, keepdims=True)
    return (x * lax.rsqrt(ms + RMS_EPS)) * w


def _rope_block(x, cos_t, sin_lo, sin_hi, half):
    return (x * cos_t + pltpu.roll(x, LANES - half, axis=1) * sin_lo
            + pltpu.roll(x, half, axis=1) * sin_hi)


def _rope_tables(seq, rope_dim):
    half = rope_dim // 2
    inv_freq = ROPE_THETA ** (-jnp.arange(half, dtype=F32) / half)
    ang = jnp.arange(seq, dtype=F32)[:, None] * inv_freq[None, :]
    cos, sin = jnp.cos(ang), jnp.sin(ang)
    pad = LANES - rope_dim
    cos_t = jnp.concatenate([cos, cos, jnp.ones((seq, pad), F32)], axis=1)
    zeros_h = jnp.zeros((seq, half), F32)
    zeros_p = jnp.zeros((seq, pad), F32)
    sin_lo = jnp.concatenate([-sin, zeros_h, zeros_p], axis=1)
    sin_hi = jnp.concatenate([zeros_h, sin, zeros_p], axis=1)
    return cos_t, sin_lo, sin_hi


def _rmsnorm_kernel(x_ref, w_ref, o_ref):
    o_ref[...] = _rms_normalize(x_ref[...], w_ref[...]).astype(o_ref.dtype)


def _rmsnorm(x, w, out_dtype, tm=1024):
    m, d = x.shape
    tm = min(tm, m)
    return pl.pallas_call(
        _rmsnorm_kernel,
        grid=(m // tm,),
        in_specs=[pl.BlockSpec((tm, d), lambda i: (i, 0)), pl.BlockSpec((1, d), lambda i: (0, 0))],
        out_specs=pl.BlockSpec((tm, d), lambda i: (i, 0)),
        out_shape=jax.ShapeDtypeStruct((m, d), out_dtype),
        compiler_params=_params(1),
        name="rmsnorm",
    )(x, w.reshape(1, d))


def _mm_kernel(a_ref, w_ref, o_ref):
    o_ref[...] = jnp.dot(a_ref[...], w_ref[...], preferred_element_type=F32).astype(o_ref.dtype)


def _mm(a, w, out_dtype, tm, tn, name):
    m, k = a.shape
    n = w.shape[1]
    tm = min(tm, m)
    return pl.pallas_call(
        _mm_kernel,
        grid=(m // tm, n // tn),
        in_specs=[pl.BlockSpec((tm, k), lambda i, j: (i, 0)), pl.BlockSpec((k, tn), lambda i, j: (0, j))],
        out_specs=pl.BlockSpec((tm, tn), lambda i, j: (i, j)),
        out_shape=jax.ShapeDtypeStruct((m, n), out_dtype),
        compiler_params=_params(2),
        name=name,
    )(a, w)


def _mm_rope_kernel(a_ref, w_ref, cos_ref, slo_ref, shi_ref, o_ref, *, half):
    acc = jnp.dot(a_ref[...], w_ref[...], preferred_element_type=F32)
    cos_t, sin_lo, sin_hi = cos_ref[...], slo_ref[...], shi_ref[...]
    for blk in range(acc.shape[1] // LANES):
        cols = slice(blk * LANES, (blk + 1) * LANES)
        o_ref[:, cols] = _rope_block(acc[:, cols], cos_t, sin_lo, sin_hi, half).astype(o_ref.dtype)


def _mm_rope(a, w, tables, seq, rope_dim, out_dtype, tm, tn, name):
    m, k = a.shape
    n = w.shape[1]
    tm = min(tm, seq)
    row_blocks = seq // tm
    tab_spec = pl.BlockSpec((tm, LANES), lambda i, j: (i % row_blocks, 0))
    return pl.pallas_call(
        functools.partial(_mm_rope_kernel, half=rope_dim // 2),
        grid=(m // tm, n // tn),
        in_specs=[pl.BlockSpec((tm, k), lambda i, j: (i, 0)), pl.BlockSpec((k, tn), lambda i, j: (0, j)),
                  tab_spec, tab_spec, tab_spec],
        out_specs=pl.BlockSpec((tm, tn), lambda i, j: (i, j)),
        out_shape=jax.ShapeDtypeStruct((m, n), out_dtype),
        compiler_params=_params(2),
        name=name,
    )(a, w, *tables)


def _mla_small_kernel(a_ref, w_ref, cos_ref, slo_ref, shi_ref, o_ref, kr_ref):
    acc = jnp.dot(a_ref[...], w_ref[...], preferred_element_type=F32)
    o_ref[...] = acc
    k_r = acc[:, MLA_Q_LORA:MLA_Q_LORA + LANES]
    kr_ref[...] = _rope_block(k_r, cos_ref[...], slo_ref[...], shi_ref[...], MLA_ROPE_DIM // 2).astype(kr_ref.dtype)


def _mla_small(u, w, tables, seq, tm=1024):
    m, k = u.shape
    tm = min(tm, seq)
    row_blocks = seq // tm
    tab_spec = pl.BlockSpec((tm, LANES), lambda i: (i % row_blocks, 0))
    return pl.pallas_call(
        _mla_small_kernel,
        grid=(m // tm,),
        in_specs=[pl.BlockSpec((tm, k), lambda i: (i, 0)), pl.BlockSpec((k, MLA_SMALL_COLS), lambda i: (0, 0)),
                  tab_spec, tab_spec, tab_spec],
        out_specs=[pl.BlockSpec((tm, MLA_SMALL_COLS), lambda i: (i, 0)),
                   pl.BlockSpec((tm, LANES), lambda i: (i, 0))],
        out_shape=[jax.ShapeDtypeStruct((m, MLA_SMALL_COLS), F32), jax.ShapeDtypeStruct((m, LANES), BF16)],
        compiler_params=_params(1),
        name="mla_in_small",
    )(u, w, *tables)


def _norm_mm_kernel(a_ref, nw_ref, w_ref, *rest, rope_half):
    if rope_half:
        cos_ref, slo_ref, shi_ref, o_ref, an_ref = rest
    else:
        o_ref, an_ref = rest

    @pl.when(pl.program_id(1) == 0)
    def _():
        an_ref[...] = _rms_normalize(a_ref[...], nw_ref[...]).astype(an_ref.dtype)

    acc = jnp.dot(an_ref[...], w_ref[...], preferred_element_type=F32)
    if not rope_half:
        o_ref[...] = acc.astype(o_ref.dtype)
        return
    cos_t, sin_lo, sin_hi = cos_ref[...], slo_ref[...], shi_ref[...]
    for blk in range(acc.shape[1] // LANES):
        cols = slice(blk * LANES, (blk + 1) * LANES)
        x = acc[:, cols]
        if blk % 2 == 1:
            x = _rope_block(x, cos_t, sin_lo, sin_hi, rope_half)
        o_ref[:, cols] = x.astype(o_ref.dtype)


def _norm_mm(a, a_col_block, k, norm_w, w, out_dtype, tm, tn, name, rope=None, seq=None):
    m = a.shape[0]
    n = w.shape[1]
    tm = min(tm, m if seq is None else seq)
    in_specs = [pl.BlockSpec((tm, k), lambda i, j: (i, a_col_block)),
                pl.BlockSpec((1, k), lambda i, j: (0, 0)),
                pl.BlockSpec((k, tn), lambda i, j: (0, j))]
    args = [a, norm_w.reshape(1, k), w]
    rope_half = 0
    if rope is not None:
        tables, rope_dim = rope
        rope_half = rope_dim // 2
        row_blocks = seq // tm
        in_specs += [pl.BlockSpec((tm, LANES), lambda i, j: (i % row_blocks, 0))] * 3
        args += list(tables)
    return pl.pallas_call(
        functools.partial(_norm_mm_kernel, rope_half=rope_half),
        grid=(m // tm, n // tn),
        in_specs=in_specs,
        out_specs=pl.BlockSpec((tm, tn), lambda i, j: (i, j)),
        out_shape=jax.ShapeDtypeStruct((m, n), out_dtype),
        scratch_shapes=[pltpu.VMEM((tm, k), BF16)],
        compiler_params=_params(2),
        name=name,
    )(*args)


def _out_proj_kernel(a_ref, w_ref, h_ref, nw_ref, *out_refs, final):
    h_new = h_ref[...] + jnp.dot(a_ref[...], w_ref[...], preferred_element_type=F32)
    if final:
        (u_ref,) = out_refs
    else:
        hn_ref, u_ref = out_refs
        hn_ref[...] = h_new
    u_ref[...] = _rms_normalize(h_new, nw_ref[...]).astype(u_ref.dtype)


def _out_proj(a, w, h, norm_w, final, tm=512):
    m, k = a.shape
    d = w.shape[1]
    tm = min(tm, m)
    row_spec = pl.BlockSpec((tm, d), lambda i: (i, 0))
    if final:
        out_specs, out_shape = row_spec, jax.ShapeDtypeStruct((m, d), F32)
    else:
        out_specs = [row_spec, row_spec]
        out_shape = [jax.ShapeDtypeStruct((m, d), F32), jax.ShapeDtypeStruct((m, d), BF16)]
    return pl.pallas_call(
        functools.partial(_out_proj_kernel, final=final),
        grid=(m // tm,),
        in_specs=[pl.BlockSpec((tm, k), lambda i: (i, 0)), pl.BlockSpec((k, d), lambda i: (0, 0)),
                  row_spec, pl.BlockSpec((1, d), lambda i: (0, 0))],
        out_specs=out_specs,
        out_shape=out_shape,
        compiler_params=_params(1),
        name="out_proj_final" if final else "out_proj",
    )(a, w, h, norm_w.reshape(1, d))


def _expand_heads(v):
    t = v.shape[0]
    cols = [jnp.broadcast_to(v[:, h:h + 1], (t, LANES)) for h in range(SSM_HEADS_PER_GROUP)]
    lane = lax.broadcasted_iota(jnp.int32, (t, LANES), 1)
    first = lane < SSM_HEAD_DIM
    return jnp.concatenate([jnp.where(first, cols[0], cols[1]), jnp.where(first, cols[2], cols[3])], axis=1)


def _cumsum_rows(x):
    n = x.shape[0]
    row = lax.broadcasted_iota(jnp.int32, x.shape, 0)
    shift = 1
    while shift < n:
        x = x + jnp.where(row >= shift, pltpu.roll(x, shift, axis=0), 0.0)
        shift *= 2
    return x


def _ssd_kernel(z_ref, x_ref, b_ref, c_ref, dt_ref, wx_ref, wb_ref, wc_ref, bx_ref, bb_ref, bc_ref,
                dtb_ref, alog_ref, dskip_ref, nw_ref, o_ref, *, seq):
    t = SSM_CHUNK
    n_chunks = seq // t

    def conv_silu(ref, w_ref, bias_ref, t0, c):
        cur = ref[pl.ds(t0, t), :].astype(F32)
        prev0 = pl.multiple_of(jnp.maximum(t0 - CONV_HALO, 0), CONV_HALO)
        prev = ref[pl.ds(prev0, CONV_HALO), :].astype(F32)
        prev = jnp.where(c > 0, prev, 0.0)
        ext = jnp.concatenate([prev, cur], axis=0)
        w = w_ref[...]
        acc = bias_ref[...] + w[SSM_CONV - 1:SSM_CONV, :] * cur
        for tap in range(SSM_CONV - 1):
            shifted = pltpu.roll(ext, SSM_CONV - 1 - tap, axis=0)[CONV_HALO:, :]
            acc = acc + w[tap:tap + 1, :] * shifted
        return _silu(acc)

    row = lax.broadcasted_iota(jnp.int32, (t, t), 0)
    col = lax.broadcasted_iota(jnp.int32, (t, t), 1)
    causal = row >= col
    head_of_lane = lax.broadcasted_iota(jnp.int32, (t, SSM_GROUP_WIDTH), 1) // SSM_HEAD_DIM
    neg_a = -jnp.exp(alog_ref[...])

    def chunk(c, state):
        t0 = pl.multiple_of(c * t, t)
        x = conv_silu(x_ref, wx_ref, bx_ref, t0, c)
        b_in = conv_silu(b_ref, wb_ref, bb_ref, t0, c)
        c_out = conv_silu(c_ref, wc_ref, bc_ref, t0, c)
        dt = jax.nn.softplus(dt_ref[pl.ds(t0, t), :] + dtb_ref[...])
        a_cum = _cumsum_rows(dt * neg_a)
        a_cum_t = a_cum.T
        a_last = a_cum[t - 1:t, :]
        exp_cum = _expand_heads(jnp.exp(a_cum))
        decay_to_end = _expand_heads(jnp.exp(a_last - a_cum))
        xdt = x * _expand_heads(dt)

        b_bf = b_in.astype(BF16)
        c_bf = c_out.astype(BF16)
        cb = lax.dot_general(c_bf, b_bf, (((1,), (1,)), ((), ())), preferred_element_type=F32)
        y = jnp.dot(c_bf, state.astype(BF16), preferred_element_type=F32) * exp_cum
        for h in range(SSM_HEADS_PER_GROUP):
            seg = jnp.broadcast_to(a_cum[:, h:h + 1], (t, t)) - jnp.broadcast_to(a_cum_t[h:h + 1, :], (t, t))
            decay = jnp.exp(jnp.where(causal, seg, NEG_BIG))
            x_h = jnp.where(head_of_lane == h, xdt, 0.0).astype(BF16)
            y = y + jnp.dot((cb * decay).astype(BF16), x_h, preferred_element_type=F32)
        new_state = state * exp_cum[t - 1:t, :] + jnp.dot(
            b_in.T.astype(BF16), (xdt * decay_to_end).astype(BF16), preferred_element_type=F32)

        y = y + dskip_ref[...] * x
        g = y * _silu(z_ref[pl.ds(t0, t), :].astype(F32))
        o_ref[pl.ds(t0, t), :] = _rms_normalize(g, nw_ref[...]).astype(o_ref.dtype)
        return new_state

    lax.fori_loop(0, n_chunks, chunk, jnp.zeros((SSM_D_STATE, SSM_GROUP_WIDTH), F32))


def _ssd_mixer(proj, dt_raw, conv_w, conv_b, dt_bias, a_log, d_skip, norm_w, batch, seq):
    m = proj.shape[0]
    gw, ns = SSM_GROUP_WIDTH, SSM_D_STATE
    x_blk0 = SSM_D_INNER // gw
    b_blk0 = 2 * SSM_D_INNER // ns
    c_blk0 = b_blk0 + SSM_N_GROUPS
    cw_b0 = SSM_D_INNER // ns
    cw_c0 = cw_b0 + SSM_N_GROUPS

    def pad_heads(v):
        out = jnp.zeros((SSM_N_GROUPS, 1, LANES), F32)
        return out.at[:, 0, :SSM_HEADS_PER_GROUP].set(v.astype(F32).reshape(SSM_N_GROUPS, SSM_HEADS_PER_GROUP))

    d_lanes = jnp.repeat(d_skip.astype(F32).reshape(SSM_N_GROUPS, SSM_HEADS_PER_GROUP), SSM_HEAD_DIM,
                         axis=1).reshape(SSM_N_GROUPS, 1, gw)
    conv_b2 = conv_b.reshape(1, -1)
    nw2 = norm_w.reshape(1, -1)
    seq_spec = lambda width, blk0: pl.BlockSpec((seq, width), lambda b, g: (b, blk0 + g))
    row_spec = lambda rows, width, blk0: pl.BlockSpec((rows, width), lambda b, g: (0, blk0 + g))
    grp_spec = lambda width: pl.BlockSpec((None, 1, width), lambda b, g: (g, 0, 0))
    return pl.pallas_call(
        functools.partial(_ssd_kernel, seq=seq),
        grid=(batch, SSM_N_GROUPS),
        in_specs=[seq_spec(gw, 0), seq_spec(gw, x_blk0), seq_spec(ns, b_blk0), seq_spec(ns, c_blk0),
                  seq_spec(LANES, 0),
                  row_spec(SSM_CONV, gw, 0), row_spec(SSM_CONV, ns, cw_b0), row_spec(SSM_CONV, ns, cw_c0),
                  row_spec(1, gw, 0), row_spec(1, ns, cw_b0), row_spec(1, ns, cw_c0),
                  grp_spec(LANES), grp_spec(LANES), grp_spec(gw), row_spec(1, gw, 0)],
        out_specs=pl.BlockSpec((seq, gw), lambda b, g: (b, g)),
        out_shape=jax.ShapeDtypeStruct((m, SSM_D_INNER), BF16),
        compiler_params=_params(2),
        name="ssd_mixer",
    )(proj, proj, proj, proj, dt_raw, conv_w, conv_w, conv_w, conv_b2, conv_b2, conv_b2,
      pad_heads(dt_bias), pad_heads(a_log), d_lanes, nw2)


def _flash_kernel(*refs, tile, shared_key, forget, heads_per_cum_block):
    refs = list(refs)
    q_ref, k_ref = refs[:2]
    pos = 2
    kr_ref = ck_ref = None
    if shared_key:
        kr_ref = refs[pos]
        pos += 1
    v_ref = refs[pos]
    pos += 1
    if forget:
        ck_ref = refs[pos]
        pos += 1
    z_ref, o_ref = refs[pos], refs[pos + 1]

    if forget:
        cum_row = pl.program_id(1) % heads_per_cum_block
    n_tiles = q_ref.shape[0] // tile
    dv = v_ref.shape[1]
    row = lax.broadcasted_iota(jnp.int32, (tile, tile), 0)
    col = lax.broadcasted_iota(jnp.int32, (tile, tile), 1)
    on_or_below_diagonal = row >= col

    for qi in range(n_tiles):
        q_rows = slice(qi * tile, (qi + 1) * tile)
        q = q_ref[q_rows, :]
        m = jnp.full((tile, 1), NEG_BIG, F32)
        l = jnp.zeros((tile, 1), F32)
        acc = jnp.zeros((tile, dv), F32)
        for j in range(qi + 1):
            k_rows = slice(j * tile, (j + 1) * tile)
            k = k_ref[k_rows, :]
            if shared_key:
                k = jnp.concatenate([k, kr_ref[k_rows, :]], axis=1)
            s = lax.dot_general(q, k, (((1,), (1,)), ((), ())), preferred_element_type=F32)
            if forget:
                s = s - ck_ref[pl.ds(cum_row, 1), k_rows]
            if j == qi:
                s = jnp.where(on_or_below_diagonal, s, NEG_BIG)
            m_new = jnp.maximum(m, jnp.max(s, axis=1, keepdims=True))
            alpha = jnp.exp2(m - m_new)
            p = jnp.exp2(s - m_new)
            l = alpha * l + jnp.sum(p, axis=1, keepdims=True)
            acc = alpha * acc + jnp.dot(p.astype(BF16), v_ref[k_rows, :], preferred_element_type=F32)
            m = m_new
        o = acc / l
        o_ref[q_rows, :] = (o * _silu(z_ref[q_rows, :].astype(F32))).astype(o_ref.dtype)


def _flash(q_arr, q_width, q_blk0, k_arr, k_blk0, v_arr, v_blk0, z_arr, z_blk0, batch, seq, heads,
           tile, name, kr_arr=None, cum_t=None):
    m = q_arr.shape[0]
    dv = LANES
    in_specs = [pl.BlockSpec((seq, q_width), lambda b, h: (b, q_blk0 + h)),
                pl.BlockSpec((seq, LANES), lambda b, h: (b, k_blk0 + h))]
    args = [q_arr, k_arr]
    if kr_arr is not None:
        in_specs.append(pl.BlockSpec((seq, LANES), lambda b, h: (b, 0)))
        args.append(kr_arr)
    in_specs.append(pl.BlockSpec((seq, dv), lambda b, h: (b, v_blk0 + h)))
    args.append(v_arr)
    cum_rows = 8
    if cum_t is not None:
        in_specs.append(pl.BlockSpec((None, cum_rows, seq), lambda b, h: (b, h // cum_rows, 0)))
        args.append(cum_t)
    in_specs.append(pl.BlockSpec((seq, dv), lambda b, h: (b, z_blk0 + h)))
    args.append(z_arr)
    return pl.pallas_call(
        functools.partial(_flash_kernel, tile=min(tile, seq), shared_key=kr_arr is not None,
                          forget=cum_t is not None, heads_per_cum_block=cum_rows),
        grid=(batch, heads),
        in_specs=in_specs,
        out_specs=pl.BlockSpec((seq, dv), lambda b, h: (b, h)),
        out_shape=jax.ShapeDtypeStruct((m, heads * dv), BF16),
        compiler_params=_params(2),
        name=name,
    )(*args)


def _forget_cum_kernel(f_ref, b_ref, o_ref):
    x = f_ref[...] + b_ref[...]
    log_f = jnp.minimum(x, 0.0) - jnp.log1p(jnp.exp(-jnp.abs(x)))
    o_ref[...] = (_cumsum_rows(log_f) * LOG2E).T


def _forget_cum(f_raw, f_bias_pad, batch, seq):
    return pl.pallas_call(
        _forget_cum_kernel,
        grid=(batch,),
        in_specs=[pl.BlockSpec((seq, LANES), lambda b: (b, 0)), pl.BlockSpec((1, LANES), lambda b: (0, 0))],
        out_specs=pl.BlockSpec((None, LANES, seq), lambda b: (b, 0, 0)),
        out_shape=jax.ShapeDtypeStruct((batch, LANES, seq), F32),
        compiler_params=_params(1),
        name="forget_cum",
    )(f_raw, f_bias_pad)


def _dilated_kernel(*refs, seq):
    n_grp = len(DIL_CONFIGS)
    qkv_refs = refs[:3 * n_grp]
    z_ref, o_ref = refs[3 * n_grp], refs[3 * n_grp + 1]
    scratch = refs[3 * n_grp + 2:]
    q_s, k_s, v_s = scratch[:3]
    acc_s = scratch[3:3 + n_grp]
    max_s = scratch[3 + n_grp:3 + 2 * n_grp]
    den_s = scratch[3 + 2 * n_grp:3 + 3 * n_grp]

    for g, (window, dil) in enumerate(DIL_CONFIGS):
        q_ref, k_ref, v_ref = qkv_refs[3 * g:3 * g + 3]
        span = window // dil
        length = seq // dil
        blk = min(span, length)
        n_blk = length // blk
        q_s[...] = q_ref[...].astype(F32)
        k_s[...] = k_ref[...].astype(F32)
        v_s[...] = v_ref[...].astype(F32)
        row = lax.broadcasted_iota(jnp.int32, (blk, 2 * blk), 0)
        col = lax.broadcasted_iota(jnp.int32, (blk, 2 * blk), 1)
        dist = row + blk - col
        in_band = (dist >= 0) & (dist <= span)
        prev_half = col < blk

        def block(idx, carry, dil=dil, blk=blk, n_blk=n_blk, in_band=in_band, prev_half=prev_half,
                  acc_ref=acc_s[g], max_ref=max_s[g], den_ref=den_s[g]):
            residue = idx // n_blk
            n = idx - residue * n_blk
            start = residue + n * (blk * dil)
            prev = jnp.maximum(start - blk * dil, residue)
            rows = lambda s0: pl.ds(s0, blk, stride=dil) if dil > 1 else pl.ds(s0, blk)
            q = q_s[rows(start), :].astype(BF16)
            k = jnp.concatenate([k_s[rows(prev), :], k_s[rows(start), :]], axis=0).astype(BF16)
            v = jnp.concatenate([v_s[rows(prev), :], v_s[rows(start), :]], axis=0).astype(BF16)
            s = lax.dot_general(q, k, (((1,), (1,)), ((), ())), preferred_element_type=F32)
            no_prev = jnp.where(n > 0, 0.0, NEG_BIG)
            s = jnp.where(in_band, s, NEG_BIG)
            s = jnp.where(prev_half, s + no_prev, s)
            m = jnp.max(s, axis=1, keepdims=True)
            p = jnp.exp2(s - m)
            den = jnp.sum(p, axis=1, keepdims=True)
            acc_ref[rows(start), :] = jnp.dot(p.astype(BF16), v, preferred_element_type=F32)
            max_ref[rows(start), :] = jnp.broadcast_to(m, (blk, LANES))
            den_ref[rows(start), :] = jnp.broadcast_to(den, (blk, LANES))
            return carry

        lax.fori_loop(0, dil * n_blk, block, 0)

    rows_per_step = min(256, seq)

    def combine(i, carry):
        r0 = pl.multiple_of(i * rows_per_step, rows_per_step)
        rs = pl.ds(r0, rows_per_step)
        m_all = max_s[0][rs, :]
        for g in range(1, n_grp):
            m_all = jnp.maximum(m_all, max_s[g][rs, :])
        num = jnp.zeros((rows_per_step, LANES), F32)
        den = jnp.zeros((rows_per_step, LANES), F32)
        for g in range(n_grp):
            w = jnp.exp2(max_s[g][rs, :] - m_all)
            num = num + w * acc_s[g][rs, :]
            den = den + w * den_s[g][rs, :]
        o_ref[rs, :] = ((num / den) * _silu(z_ref[rs, :].astype(F32))).astype(o_ref.dtype)
        return carry

    lax.fori_loop(0, seq // rows_per_step, combine, 0)


def _dilated_mixer(qk, vz, batch, seq):
    m = qk.shape[0]
    n_grp = len(DIL_CONFIGS)
    hb = DIL_N_HEADS
    in_specs, args = [], []
    for g in range(n_grp):
        in_specs += [pl.BlockSpec((seq, LANES), lambda b, h, g=g: (b, 2 * hb * g + h)),
                     pl.BlockSpec((seq, LANES), lambda b, h, g=g: (b, 2 * hb * g + hb + h)),
                     pl.BlockSpec((seq, LANES), lambda b, h, g=g: (b, hb * g + h))]
        args += [qk, qk, vz]
    in_specs.append(pl.BlockSpec((seq, LANES), lambda b, h: (b, hb * n_grp + h)))
    args.append(vz)
    return pl.pallas_call(
        functools.partial(_dilated_kernel, seq=seq),
        grid=(batch, DIL_N_HEADS),
        in_specs=in_specs,
        out_specs=pl.BlockSpec((seq, LANES), lambda b, h: (b, h)),
        out_shape=jax.ShapeDtypeStruct((m, DIL_WIDTH), BF16),
        scratch_shapes=[pltpu.VMEM((seq, LANES), F32)] * (3 + 3 * n_grp),
        compiler_params=_params(2),
        name="dilated_mixer",
    )(*args)


def _ssd_layer(u, h, next_norm_w, final, in_w, conv_w, conv_b, dt_bias, a_log, d_skip, norm_w, out_w,
               batch, seq):
    w_main = in_w[:, :SSM_MAIN_COLS].astype(BF16)
    w_dt = in_w[:, SSM_MAIN_COLS:].reshape(D_MODEL, SSM_N_GROUPS, SSM_HEADS_PER_GROUP)
    w_dt = jnp.zeros((D_MODEL, SSM_N_GROUPS, LANES), F32).at[:, :, :SSM_HEADS_PER_GROUP].set(w_dt)
    w_dt = w_dt.reshape(D_MODEL, SSM_N_GROUPS * LANES).astype(BF16)
    proj = _mm(u, w_main, BF16, 1024, 512, "ssd_in_proj")
    dt_raw = _mm(u, w_dt, F32, 1024, 1024, "ssd_dt_proj")
    y = _ssd_mixer(proj, dt_raw, conv_w, conv_b, dt_bias, a_log, d_skip, norm_w, batch, seq)
    return _out_proj(y, out_w.astype(BF16), h, next_norm_w, final)


def _mla_layer(u, h, next_norm_w, final, in_w, q_norm_w, kv_norm_w, uq_w, ukv_w, out_w, batch, seq):
    o1 = MLA_Q_LORA
    o2 = o1 + MLA_KV_LORA
    o3 = o2 + MLA_ROPE_DIM
    w_small = jnp.concatenate([in_w[:, :o1], in_w[:, o2:o3], jnp.zeros((D_MODEL, LANES - MLA_ROPE_DIM), F32),
                               in_w[:, o1:o2]], axis=1).astype(BF16)
    w_z = in_w[:, o3:].astype(BF16)
    q_scale = MLA_QK_DIM ** -0.5 * LOG2E
    w_q = (uq_w * q_scale).reshape(MLA_Q_LORA, MLA_N_HEADS, MLA_QK_DIM)
    w_q = jnp.pad(w_q, ((0, 0), (0, 0), (0, MLA_Q_PAD - MLA_QK_DIM))).reshape(MLA_Q_LORA, -1).astype(BF16)
    w_kv = ukv_w.reshape(MLA_KV_LORA, MLA_N_HEADS, 2, MLA_NOPE_DIM).transpose(0, 2, 1, 3)
    w_kv = w_kv.reshape(MLA_KV_LORA, -1).astype(BF16)
    tables = _rope_tables(seq, MLA_ROPE_DIM)

    small, k_rope = _mla_small(u, w_small, tables, seq)
    z = _mm(u, w_z, BF16, 1024, 512, "mla_gate_proj")
    q = _norm_mm(small, 0, MLA_Q_LORA, q_norm_w, w_q, BF16, 1024, 512, "mla_q_proj",
                 rope=(tables, MLA_ROPE_DIM), seq=seq)
    kv = _norm_mm(small, (MLA_SMALL_COLS - MLA_KV_LORA) // MLA_KV_LORA, MLA_KV_LORA, kv_norm_w, w_kv, BF16,
                  1024, 512, "mla_kv_proj")
    o = _flash(q, MLA_Q_PAD, 0, kv, 0, kv, MLA_N_HEADS, z, 0, batch, seq, MLA_N_HEADS, 512, "mla_attention",
               kr_arr=k_rope)
    return _out_proj(o, out_w.astype(BF16), h, next_norm_w, final)


def _fox_layer(u, h, next_norm_w, final, in_w, f_bias, out_w, batch, seq):
    w = FOX_WIDTH
    q_scale = FOX_HEAD_DIM ** -0.5 * LOG2E
    w_main = jnp.concatenate([in_w[:, :w] * q_scale, in_w[:, w:3 * w], in_w[:, 3 * w + FOX_N_HEADS:]],
                             axis=1).astype(BF16)
    w_f = jnp.pad(in_w[:, 3 * w:3 * w + FOX_N_HEADS], ((0, 0), (0, LANES - FOX_N_HEADS))).astype(BF16)
    f_bias_pad = jnp.pad(f_bias.astype(F32), (0, LANES - FOX_N_HEADS)).reshape(1, LANES)
    proj = _mm(u, w_main, BF16, 1024, 512, "fox_in_proj")
    f_raw = _mm(u, w_f, F32, 1024, LANES, "fox_forget_proj")
    cum_t = _forget_cum(f_raw, f_bias_pad, batch, seq)
    hh = FOX_N_HEADS
    o = _flash(proj, LANES, 0, proj, hh, proj, 2 * hh, proj, 3 * hh, batch, seq, hh, 512, "fox_attention",
               cum_t=cum_t)
    return _out_proj(o, out_w.astype(BF16), h, next_norm_w, final)


def _dilated_layer(u, h, next_norm_w, final, in_w, out_w, batch, seq):
    w = DIL_WIDTH
    q_scale = DIL_HEAD_DIM ** -0.5 * LOG2E
    qk_cols, v_cols = [], []
    for gi in range(len(DIL_CONFIGS)):
        base = 3 * w * gi
        qk_cols += [in_w[:, base:base + w] * q_scale, in_w[:, base + w:base + 2 * w]]
        v_cols.append(in_w[:, base + 2 * w:base + 3 * w])
    w_qk = jnp.concatenate(qk_cols, axis=1).astype(BF16)
    w_vz = jnp.concatenate(v_cols + [in_w[:, 3 * w * len(DIL_CONFIGS):]], axis=1).astype(BF16)
    tables = _rope_tables(seq, DIL_ROPE_DIM)
    qk = _mm_rope(u, w_qk, tables, seq, DIL_ROPE_DIM, BF16, 1024, 512, "dilated_qk_proj")
    vz = _mm(u, w_vz, BF16, 1024, 512, "dilated_vz_proj")
    o = _dilated_mixer(qk, vz, batch, seq)
    return _out_proj(o, out_w.astype(BF16), h, next_norm_w, final)


def kernel(x, norm_w, final_norm_w, ssm_in_w, ssm_conv_w, ssm_conv_b, ssm_dt_bias, ssm_A_log, ssm_D,
           ssm_norm_w, ssm_out_w, mla_in_w, mla_q_norm_w, mla_kv_norm_w, mla_uq_w, mla_ukv_w, mla_out_w,
           fox_in_w, fox_f_bias, fox_out_w, dil_in_w, dil_out_w):
    batch, seq, d = x.shape
    depth = norm_w.shape[0]
    h = x.reshape(batch * seq, d)
    u = _rmsnorm(h, norm_w[0], BF16)
    for i in range(depth):
        kind, j = i % 4, i // 4
        final = i == depth - 1
        next_w = final_norm_w if final else norm_w[i + 1]
        if kind == 0:
            res = _ssd_layer(u, h, next_w, final, ssm_in_w[j], ssm_conv_w[j], ssm_conv_b[j], ssm_dt_bias[j],
                             ssm_A_log[j], ssm_D[j], ssm_norm_w[j], ssm_out_w[j], batch, seq)
        elif kind == 1:
            res = _mla_layer(u, h, next_w, final, mla_in_w[j], mla_q_norm_w[j], mla_kv_norm_w[j], mla_uq_w[j],
                             mla_ukv_w[j], mla_out_w[j], batch, seq)
        elif kind == 2:
            res = _fox_layer(u, h, next_w, final, fox_in_w[j], fox_f_bias[j], fox_out_w[j], batch, seq)
        else:
            res = _dilated_layer(u, h, next_w, final, dil_in_w[j], dil_out_w[j], batch, seq)
        if final:
            return res.reshape(batch, seq, d)
        h, u = res
```

```python
import functools
import math

import jax
import jax.numpy as jnp
from jax import lax
from jax.experimental import pallas as pl
from jax.experimental.pallas import tpu as pltpu

F32 = jnp.float32
BF16 = jnp.bfloat16

D_MODEL = 1024
RMS_EPS = 1e-6
ROPE_THETA = 500000.0
LOG2E = 1.4426950408889634
NEG_BIG = -1e30

LANES = 128

SSM_D_INNER = 2048
SSM_HEAD_DIM = 64
SSM_N_HEADS = 32
SSM_D_STATE = 128
SSM_N_GROUPS = 8
SSM_HEADS_PER_GROUP = SSM_N_HEADS // SSM_N_GROUPS
SSM_GROUP_WIDTH = SSM_HEADS_PER_GROUP * SSM_HEAD_DIM
SSM_CHUNK = 128
SSM_CONV = 4
SSM_BC_DIM = SSM_N_GROUPS * SSM_D_STATE
SSM_MAIN_COLS = 2 * SSM_D_INNER + 2 * SSM_BC_DIM
CONV_HALO = 16

MLA_N_HEADS = 16
MLA_Q_LORA = 384
MLA_KV_LORA = 256
MLA_NOPE_DIM = 128
MLA_ROPE_DIM = 64
MLA_V_DIM = 128
MLA_QK_DIM = MLA_NOPE_DIM + MLA_ROPE_DIM
MLA_WIDTH = MLA_N_HEADS * MLA_V_DIM
MLA_Q_PAD = 2 * LANES
MLA_SMALL_COLS = 768

FOX_N_HEADS = 16
FOX_HEAD_DIM = 128
FOX_WIDTH = FOX_N_HEADS * FOX_HEAD_DIM

DIL_CONFIGS = ((128, 1), (512, 4), (2048, 16))
DIL_N_HEADS = 8
DIL_HEAD_DIM = 128
DIL_WIDTH = DIL_N_HEADS * DIL_HEAD_DIM
DIL_ROPE_DIM = DIL_HEAD_DIM // 4

VMEM_LIMIT_BYTES = 56 * 1024 * 1024
PROJ_ROW_TILE = 2048
PROJ_COL_TILE = 1024
OUT_PROJ_ROW_TILE = 1024


def _params(n_axes):
    return pltpu.CompilerParams(dimension_semantics=("arbitrary",) * n_axes,
                                vmem_limit_bytes=VMEM_LIMIT_BYTES)


def _silu(x):
    return x * jax.nn.sigmoid(x)


def _rms_normalize(x, w):
    ms = jnp.mean(x * x, axis=-1, keepdims=True)
    return (x * lax.rsqrt(ms + RMS_EPS)) * w


ROPE_PARTNER_SHIFT = LANES // 2


def _rope_block(x, cos_t, sin_t):
    return x * cos_t + pltpu.roll(x, ROPE_PARTNER_SHIFT, axis=1) * sin_t


def _rope_split_perm(rope_dim, width=LANES):
    half = rope_dim // 2
    first = list(range(half))
    second = list(range(half, rope_dim))
    rest = list(range(rope_dim, width))
    n_fill = ROPE_PARTNER_SHIFT - half
    return jnp.array(first + rest[:n_fill] + second + rest[n_fill:], jnp.int32)


def _rope_tables(seq, rope_dim):
    half = rope_dim // 2
    inv_freq = ROPE_THETA ** (-jnp.arange(half, dtype=F32) / half)
    ang = jnp.arange(seq, dtype=F32)[:, None] * inv_freq[None, :]
    cos, sin = jnp.cos(ang), jnp.sin(ang)
    ones = jnp.ones((seq, ROPE_PARTNER_SHIFT - half), F32)
    zeros = jnp.zeros((seq, ROPE_PARTNER_SHIFT - half), F32)
    cos_t = jnp.concatenate([cos, ones, cos, ones], axis=1)
    sin_t = jnp.concatenate([-sin, zeros, sin, zeros], axis=1)
    return cos_t, sin_t


def _rmsnorm_kernel(x_ref, w_ref, o_ref):
    o_ref[...] = _rms_normalize(x_ref[...], w_ref[...]).astype(o_ref.dtype)


def _rmsnorm(x, w, out_dtype, tm=1024):
    m, d = x.shape
    tm = min(tm, m)
    return pl.pallas_call(
        _rmsnorm_kernel,
        grid=(m // tm,),
        in_specs=[pl.BlockSpec((tm, d), lambda i: (i, 0)), pl.BlockSpec((1, d), lambda i: (0, 0))],
        out_specs=pl.BlockSpec((tm, d), lambda i: (i, 0)),
        out_shape=jax.ShapeDtypeStruct((m, d), out_dtype),
        compiler_params=_params(1),
        name="rmsnorm",
    )(x, w.reshape(1, d))


def _mm_kernel(a_ref, w_ref, o_ref):
    o_ref[...] = jnp.dot(a_ref[...], w_ref[...], preferred_element_type=F32).astype(o_ref.dtype)


def _mm(a, w, out_dtype, tm, tn, name):
    m, k = a.shape
    n = w.shape[1]
    tm = min(tm, m)
    tn = min(tn, n)
    return pl.pallas_call(
        _mm_kernel,
        grid=(m // tm, n // tn),
        in_specs=[pl.BlockSpec((tm, k), lambda i, j: (i, 0)), pl.BlockSpec((k, tn), lambda i, j: (0, j))],
        out_specs=pl.BlockSpec((tm, tn), lambda i, j: (i, j)),
        out_shape=jax.ShapeDtypeStruct((m, n), out_dtype),
        compiler_params=_params(2),
        name=name,
    )(a, w)


def _mm_rope_kernel(a_ref, w_ref, cos_ref, sin_ref, o_ref):
    acc = jnp.dot(a_ref[...], w_ref[...], preferred_element_type=F32)
    cos_t, sin_t = cos_ref[...], sin_ref[...]
    for blk in range(acc.shape[1] // LANES):
        cols = slice(blk * LANES, (blk + 1) * LANES)
        o_ref[:, cols] = _rope_block(acc[:, cols], cos_t, sin_t).astype(o_ref.dtype)


def _mm_rope(a, w, tables, seq, out_dtype, tm, tn, name):
    m, k = a.shape
    n = w.shape[1]
    tm = min(tm, seq)
    row_blocks = seq // tm
    tab_spec = pl.BlockSpec((tm, LANES), lambda i, j: (i % row_blocks, 0))
    return pl.pallas_call(
        _mm_rope_kernel,
        grid=(m // tm, n // tn),
        in_specs=[pl.BlockSpec((tm, k), lambda i, j: (i, 0)), pl.BlockSpec((k, tn), lambda i, j: (0, j)),
                  tab_spec, tab_spec],
        out_specs=pl.BlockSpec((tm, tn), lambda i, j: (i, j)),
        out_shape=jax.ShapeDtypeStruct((m, n), out_dtype),
        compiler_params=_params(2),
        name=name,
    )(a, w, *tables)


def _mla_small_kernel(a_ref, w_ref, cos_ref, sin_ref, o_ref, kr_ref):
    acc = jnp.dot(a_ref[...], w_ref[...], preferred_element_type=F32)
    o_ref[...] = acc
    k_r = acc[:, MLA_Q_LORA:MLA_Q_LORA + LANES]
    kr_ref[...] = _rope_block(k_r, cos_ref[...], sin_ref[...]).astype(kr_ref.dtype)


def _mla_small(u, w, tables, seq, tm=1024):
    m, k = u.shape
    tm = min(tm, seq)
    row_blocks = seq // tm
    tab_spec = pl.BlockSpec((tm, LANES), lambda i: (i % row_blocks, 0))
    return pl.pallas_call(
        _mla_small_kernel,
        grid=(m // tm,),
        in_specs=[pl.BlockSpec((tm, k), lambda i: (i, 0)), pl.BlockSpec((k, MLA_SMALL_COLS), lambda i: (0, 0)),
                  tab_spec, tab_spec],
        out_specs=[pl.BlockSpec((tm, MLA_SMALL_COLS), lambda i: (i, 0)),
                   pl.BlockSpec((tm, LANES), lambda i: (i, 0))],
        out_shape=[jax.ShapeDtypeStruct((m, MLA_SMALL_COLS), F32), jax.ShapeDtypeStruct((m, LANES), BF16)],
        compiler_params=_params(1),
        name="mla_in_small",
    )(u, w, *tables)


def _norm_mm_kernel(a_ref, nw_ref, w_ref, *rest, rope):
    if rope:
        cos_ref, sin_ref, o_ref, an_ref = rest
    else:
        o_ref, an_ref = rest

    @pl.when(pl.program_id(1) == 0)
    def _():
        an_ref[...] = _rms_normalize(a_ref[...], nw_ref[...]).astype(an_ref.dtype)

    acc = jnp.dot(an_ref[...], w_ref[...], preferred_element_type=F32)
    if not rope:
        o_ref[...] = acc.astype(o_ref.dtype)
        return
    cos_t, sin_t = cos_ref[...], sin_ref[...]
    for blk in range(acc.shape[1] // LANES):
        cols = slice(blk * LANES, (blk + 1) * LANES)
        x = acc[:, cols]
        if blk % 2 == 1:
            x = _rope_block(x, cos_t, sin_t)
        o_ref[:, cols] = x.astype(o_ref.dtype)


def _norm_mm(a, a_col_block, k, norm_w, w, out_dtype, tm, tn, name, rope_tables=None, seq=None):
    m = a.shape[0]
    n = w.shape[1]
    tm = min(tm, m if seq is None else seq)
    in_specs = [pl.BlockSpec((tm, k), lambda i, j: (i, a_col_block)),
                pl.BlockSpec((1, k), lambda i, j: (0, 0)),
                pl.BlockSpec((k, tn), lambda i, j: (0, j))]
    args = [a, norm_w.reshape(1, k), w]
    if rope_tables is not None:
        row_blocks = seq // tm
        in_specs += [pl.BlockSpec((tm, LANES), lambda i, j: (i % row_blocks, 0))] * 2
        args += list(rope_tables)
    return pl.pallas_call(
        functools.partial(_norm_mm_kernel, rope=rope_tables is not None),
        grid=(m // tm, n // tn),
        in_specs=in_specs,
        out_specs=pl.BlockSpec((tm, tn), lambda i, j: (i, j)),
        out_shape=jax.ShapeDtypeStruct((m, n), out_dtype),
        scratch_shapes=[pltpu.VMEM((tm, k), BF16)],
        compiler_params=_params(2),
        name=name,
    )(*args)


def _out_proj_kernel(a_ref, w_ref, h_ref, nw_ref, *out_refs, final):
    h_new = h_ref[...] + jnp.dot(a_ref[...], w_ref[...], preferred_element_type=F32)
    if final:
        (u_ref,) = out_refs
    else:
        hn_ref, u_ref = out_refs
        hn_ref[...] = h_new
    u_ref[...] = _rms_normalize(h_new, nw_ref[...]).astype(u_ref.dtype)


def _out_proj(a, w, h, norm_w, final, tm=OUT_PROJ_ROW_TILE):
    m, k = a.shape
    d = w.shape[1]
    tm = min(tm, m)
    row_spec = pl.BlockSpec((tm, d), lambda i: (i, 0))
    if final:
        out_specs, out_shape = row_spec, jax.ShapeDtypeStruct((m, d), F32)
    else:
        out_specs = [row_spec, row_spec]
        out_shape = [jax.ShapeDtypeStruct((m, d), F32), jax.ShapeDtypeStruct((m, d), BF16)]
    return pl.pallas_call(
        functools.partial(_out_proj_kernel, final=final),
        grid=(m // tm,),
        in_specs=[pl.BlockSpec((tm, k), lambda i: (i, 0)), pl.BlockSpec((k, d), lambda i: (0, 0)),
                  row_spec, pl.BlockSpec((1, d), lambda i: (0, 0))],
        out_specs=out_specs,
        out_shape=out_shape,
        compiler_params=_params(1),
        name="out_proj_final" if final else "out_proj",
    )(a, w, h, norm_w.reshape(1, d))


def _expand_heads(v):
    t = v.shape[0]
    cols = [jnp.broadcast_to(v[:, h:h + 1], (t, LANES)) for h in range(SSM_HEADS_PER_GROUP)]
    lane = lax.broadcasted_iota(jnp.int32, (t, LANES), 1)
    first = lane < SSM_HEAD_DIM
    return jnp.concatenate([jnp.where(first, cols[0], cols[1]), jnp.where(first, cols[2], cols[3])], axis=1)


def _cumsum_rows(x):
    n = x.shape[0]
    row = lax.broadcasted_iota(jnp.int32, x.shape, 0)
    shift = 1
    while shift < n:
        x = x + jnp.where(row >= shift, pltpu.roll(x, shift, axis=0), 0.0)
        shift *= 2
    return x


def _ssd_kernel(z_ref, x_ref, b_ref, c_ref, dt_ref, wx_ref, wb_ref, wc_ref, bx_ref, bb_ref, bc_ref,
                dtb_ref, alog_ref, dskip_ref, nw_ref, o_ref, *, seq):
    t = SSM_CHUNK
    n_chunks = seq // t

    gw, ns = SSM_GROUP_WIDTH, SSM_D_STATE
    conv_w = jnp.concatenate([wx_ref[...], wb_ref[...], wc_ref[...]], axis=1)
    conv_b = jnp.concatenate([bx_ref[...], bb_ref[...], bc_ref[...]], axis=1)
    ext_rows = 2 * t
    n_shift = SSM_CONV - 1
    sel_row = lax.broadcasted_iota(jnp.int32, (n_shift * t, ext_rows), 0)
    sel_col = lax.broadcasted_iota(jnp.int32, (n_shift * t, ext_rows), 1)
    shift_select = (sel_col == CONV_HALO + sel_row % t - (n_shift - sel_row // t)).astype(BF16)
    ext_fill = jnp.zeros((ext_rows - t - CONV_HALO, gw + 2 * ns), BF16)

    def conv_silu(t0, c):
        rows = pl.ds(t0, t)
        cur = jnp.concatenate([x_ref[rows, :], b_ref[rows, :], c_ref[rows, :]], axis=1)
        halo = pl.ds(pl.multiple_of(jnp.maximum(t0 - CONV_HALO, 0), CONV_HALO), CONV_HALO)
        prev = jnp.concatenate([x_ref[halo, :], b_ref[halo, :], c_ref[halo, :]], axis=1)
        prev = jnp.where(c > 0, prev, jnp.zeros_like(prev))
        ext = jnp.concatenate([prev, cur, ext_fill], axis=0)
        shifted = jnp.dot(shift_select, ext, preferred_element_type=F32)
        acc = conv_b + conv_w[n_shift:n_shift + 1, :] * cur.astype(F32)
        for tap in range(n_shift):
            acc = acc + conv_w[tap:tap + 1, :] * shifted[tap * t:(tap + 1) * t, :]
        act = _silu(acc)
        return act[:, :gw], act[:, gw:gw + ns], act[:, gw + ns:]

    row = lax.broadcasted_iota(jnp.int32, (t, t), 0)
    col = lax.broadcasted_iota(jnp.int32, (t, t), 1)
    causal = row >= col
    head_of_lane = lax.broadcasted_iota(jnp.int32, (t, SSM_GROUP_WIDTH), 1) // SSM_HEAD_DIM
    neg_a = -jnp.exp(alog_ref[...])

    def chunk(c, state):
        t0 = pl.multiple_of(c * t, t)
        x, b_in, c_out = conv_silu(t0, c)
        dt = jax.nn.softplus(dt_ref[pl.ds(t0, t), :] + dtb_ref[...])
        a_cum = _cumsum_rows(dt * neg_a)
        a_cum_t = a_cum.T
        a_cum_wide = _expand_heads(a_cum)
        exp_cum = jnp.exp(a_cum_wide)
        decay_to_end = jnp.exp(a_cum_wide[t - 1:t, :] - a_cum_wide)
        xdt = x * _expand_heads(dt)

        b_bf = b_in.astype(BF16)
        c_bf = c_out.astype(BF16)
        cb = lax.dot_general(c_bf, b_bf, (((1,), (1,)), ((), ())), preferred_element_type=F32)
        y = jnp.dot(c_bf, state.astype(BF16), preferred_element_type=F32) * exp_cum
        for h in range(SSM_HEADS_PER_GROUP):
            seg = jnp.broadcast_to(a_cum[:, h:h + 1], (t, t)) - jnp.broadcast_to(a_cum_t[h:h + 1, :], (t, t))
            decay = jnp.exp(jnp.where(causal, seg, NEG_BIG))
            x_h = jnp.where(head_of_lane == h, xdt, 0.0).astype(BF16)
            y = y + jnp.dot((cb * decay).astype(BF16), x_h, preferred_element_type=F32)
        new_state = state * exp_cum[t - 1:t, :] + jnp.dot(
            b_in.T.astype(BF16), (xdt * decay_to_end).astype(BF16), preferred_element_type=F32)

        y = y + dskip_ref[...] * x
        g = y * _silu(z_ref[pl.ds(t0, t), :].astype(F32))
        o_ref[pl.ds(t0, t), :] = _rms_normalize(g, nw_ref[...]).astype(o_ref.dtype)
        return new_state

    lax.fori_loop(0, n_chunks, chunk, jnp.zeros((SSM_D_STATE, SSM_GROUP_WIDTH), F32),
                  unroll=2 if n_chunks % 2 == 0 else 1)


def _ssd_mixer(proj, dt_raw, conv_w, conv_b, dt_bias, a_log, d_skip, norm_w, batch, seq):
    m = proj.shape[0]
    gw, ns = SSM_GROUP_WIDTH, SSM_D_STATE
    x_blk0 = SSM_D_INNER // gw
    b_blk0 = 2 * SSM_D_INNER // ns
    c_blk0 = b_blk0 + SSM_N_GROUPS
    cw_b0 = SSM_D_INNER // ns
    cw_c0 = cw_b0 + SSM_N_GROUPS

    def pad_heads(v):
        out = jnp.zeros((SSM_N_GROUPS, 1, LANES), F32)
        return out.at[:, 0, :SSM_HEADS_PER_GROUP].set(v.astype(F32).reshape(SSM_N_GROUPS, SSM_HEADS_PER_GROUP))

    d_lanes = jnp.repeat(d_skip.astype(F32).reshape(SSM_N_GROUPS, SSM_HEADS_PER_GROUP), SSM_HEAD_DIM,
                         axis=1).reshape(SSM_N_GROUPS, 1, gw)
    conv_b2 = conv_b.reshape(1, -1)
    nw2 = norm_w.reshape(1, -1)
    seq_spec = lambda width, blk0: pl.BlockSpec((seq, width), lambda b, g: (b, blk0 + g))
    row_spec = lambda rows, width, blk0: pl.BlockSpec((rows, width), lambda b, g: (0, blk0 + g))
    grp_spec = lambda width: pl.BlockSpec((None, 1, width), lambda b, g: (g, 0, 0))
    return pl.pallas_call(
        functools.partial(_ssd_kernel, seq=seq),
        grid=(batch, SSM_N_GROUPS),
        in_specs=[seq_spec(gw, 0), seq_spec(gw, x_blk0), seq_spec(ns, b_blk0), seq_spec(ns, c_blk0),
                  seq_spec(LANES, 0),
                  row_spec(SSM_CONV, gw, 0), row_spec(SSM_CONV, ns, cw_b0), row_spec(SSM_CONV, ns, cw_c0),
                  row_spec(1, gw, 0), row_spec(1, ns, cw_b0), row_spec(1, ns, cw_c0),
                  grp_spec(LANES), grp_spec(LANES), grp_spec(gw), row_spec(1, gw, 0)],
        out_specs=pl.BlockSpec((seq, gw), lambda b, g: (b, g)),
        out_shape=jax.ShapeDtypeStruct((m, SSM_D_INNER), BF16),
        compiler_params=_params(2),
        name="ssd_mixer",
    )(proj, proj, proj, proj, dt_raw, conv_w, conv_w, conv_w, conv_b2, conv_b2, conv_b2,
      pad_heads(dt_bias), pad_heads(a_log), d_lanes, nw2)


def _flash_kernel(*refs, tile, shared_key, forget, heads_per_cum_block):
    refs = list(refs)
    q_ref, k_ref = refs[:2]
    pos = 2
    kr_ref = ck_ref = None
    if shared_key:
        kr_ref = refs[pos]
        pos += 1
    v_ref = refs[pos]
    pos += 1
    if forget:
        ck_ref = refs[pos]
        pos += 1
    z_ref, o_ref = refs[pos], refs[pos + 1]

    if forget:
        cum_row = pl.program_id(1) % heads_per_cum_block
    n_tiles = q_ref.shape[0] // tile
    dv = v_ref.shape[1]
    row = lax.broadcasted_iota(jnp.int32, (tile, tile), 0)
    col = lax.broadcasted_iota(jnp.int32, (tile, tile), 1)
    on_or_below_diagonal = row >= col

    for qi in range(n_tiles):
        q_rows = slice(qi * tile, (qi + 1) * tile)
        q = q_ref[q_rows, :]
        m = jnp.full((tile, 1), NEG_BIG, F32)
        l = jnp.zeros((tile, 1), F32)
        acc = jnp.zeros((tile, dv), F32)
        for j in range(qi + 1):
            k_rows = slice(j * tile, (j + 1) * tile)
            k = k_ref[k_rows, :]
            if shared_key:
                k = jnp.concatenate([k, kr_ref[k_rows, :]], axis=1)
            s = lax.dot_general(q, k, (((1,), (1,)), ((), ())), preferred_element_type=F32)
            if forget:
                s = s - ck_ref[pl.ds(cum_row, 1), k_rows]
            if j == qi:
                s = jnp.where(on_or_below_diagonal, s, NEG_BIG)
            m_new = jnp.maximum(m, jnp.max(s, axis=1, keepdims=True))
            alpha = jnp.exp2(m - m_new)
            p = jnp.exp2(s - m_new)
            l = alpha * l + jnp.sum(p, axis=1, keepdims=True)
            acc = alpha * acc + jnp.dot(p.astype(BF16), v_ref[k_rows, :], preferred_element_type=F32)
            m = m_new
        o = acc / l
        o_ref[q_rows, :] = (o * _silu(z_ref[q_rows, :].astype(F32))).astype(o_ref.dtype)


def _flash(q_arr, q_width, q_blk0, k_arr, k_blk0, v_arr, v_blk0, z_arr, z_blk0, batch, seq, heads,
           tile, name, kr_arr=None, cum_t=None):
    m = q_arr.shape[0]
    dv = LANES
    in_specs = [pl.BlockSpec((seq, q_width), lambda b, h: (b, q_blk0 + h)),
                pl.BlockSpec((seq, LANES), lambda b, h: (b, k_blk0 + h))]
    args = [q_arr, k_arr]
    if kr_arr is not None:
        in_specs.append(pl.BlockSpec((seq, LANES), lambda b, h: (b, 0)))
        args.append(kr_arr)
    in_specs.append(pl.BlockSpec((seq, dv), lambda b, h: (b, v_blk0 + h)))
    args.append(v_arr)
    cum_rows = 8
    if cum_t is not None:
        in_specs.append(pl.BlockSpec((None, cum_rows, seq), lambda b, h: (b, h // cum_rows, 0)))
        args.append(cum_t)
    in_specs.append(pl.BlockSpec((seq, dv), lambda b, h: (b, z_blk0 + h)))
    args.append(z_arr)
    return pl.pallas_call(
        functools.partial(_flash_kernel, tile=min(tile, seq), shared_key=kr_arr is not None,
                          forget=cum_t is not None, heads_per_cum_block=cum_rows),
        grid=(batch, heads),
        in_specs=in_specs,
        out_specs=pl.BlockSpec((seq, dv), lambda b, h: (b, h)),
        out_shape=jax.ShapeDtypeStruct((m, heads * dv), BF16),
        compiler_params=_params(2),
        name=name,
    )(*args)


def _forget_cum_kernel(f_ref, b_ref, o_ref):
    x = f_ref[...] + b_ref[...]
    log_f = jnp.minimum(x, 0.0) - jnp.log1p(jnp.exp(-jnp.abs(x)))
    o_ref[...] = (_cumsum_rows(log_f) * LOG2E).T


def _forget_cum(f_raw, f_bias_pad, batch, seq):
    return pl.pallas_call(
        _forget_cum_kernel,
        grid=(batch,),
        in_specs=[pl.BlockSpec((seq, LANES), lambda b: (b, 0)), pl.BlockSpec((1, LANES), lambda b: (0, 0))],
        out_specs=pl.BlockSpec((None, LANES, seq), lambda b: (b, 0, 0)),
        out_shape=jax.ShapeDtypeStruct((batch, LANES, seq), F32),
        compiler_params=_params(1),
        name="forget_cum",
    )(f_raw, f_bias_pad)


def _dilated_kernel(*refs, seq):
    n_grp = len(DIL_CONFIGS)
    qkv_refs = refs[:3 * n_grp]
    z_ref, o_ref = refs[3 * n_grp], refs[3 * n_grp + 1]
    scratch = refs[3 * n_grp + 2:]
    stage = scratch[:3]
    out_s = scratch[3:3 + n_grp]
    lse_s = scratch[3 + n_grp:3 + 2 * n_grp]

    for g, (window, dil) in enumerate(DIL_CONFIGS):
        qkv = qkv_refs[3 * g:3 * g + 3]
        span = window // dil
        length = seq // dil
        blk = min(span, length)
        n_blk = length // blk
        if dil > 1:
            for src, dst in zip(qkv, stage):
                dst[...] = src[...].astype(F32)

        def subseq(which, start, count, dil=dil, qkv=qkv):
            if dil == 1:
                return qkv[which][start:start + count, :]
            return stage[which][pl.ds(start, count, stride=dil), :].astype(BF16)

        row1 = lax.broadcasted_iota(jnp.int32, (blk, blk), 0)
        col1 = lax.broadcasted_iota(jnp.int32, (blk, blk), 1)
        band_first = (row1 >= col1) & (row1 - col1 <= span)
        row2 = lax.broadcasted_iota(jnp.int32, (blk, 2 * blk), 0)
        col2 = lax.broadcasted_iota(jnp.int32, (blk, 2 * blk), 1)
        dist = row2 + blk - col2
        band = (dist >= 0) & (dist <= span)

        for residue in range(dil):
            for n in range(n_blk):
                start = residue + n * blk * dil
                q = subseq(0, start, blk)
                if n == 0:
                    k, v, valid = subseq(1, start, blk), subseq(2, start, blk), band_first
                else:
                    first = start - blk * dil
                    k, v, valid = subseq(1, first, 2 * blk), subseq(2, first, 2 * blk), band
                s = lax.dot_general(q, k, (((1,), (1,)), ((), ())), preferred_element_type=F32)
                s = jnp.where(valid, s, NEG_BIG)
                m = jnp.max(s, axis=1, keepdims=True)
                p = jnp.exp2(s - m)
                den = jnp.sum(p, axis=1, keepdims=True)
                o = jnp.dot(p.astype(BF16), v, preferred_element_type=F32) * (1.0 / den)
                lse = jnp.broadcast_to(m + jnp.log2(den), (blk, LANES))
                if dil == 1:
                    out_s[g][start:start + blk, :] = o
                    lse_s[g][start:start + blk, :] = lse
                else:
                    out_s[g][pl.ds(start, blk, stride=dil), :] = o
                    lse_s[g][pl.ds(start, blk, stride=dil), :] = lse

    rows_per_step = min(256, seq)
    for i in range(seq // rows_per_step):
        rs = slice(i * rows_per_step, (i + 1) * rows_per_step)
        lses = [lse_s[g][rs, :] for g in range(n_grp)]
        lse_max = functools.reduce(jnp.maximum, lses)
        weights = [jnp.exp2(l - lse_max) for l in lses]
        num = sum(w * out_s[g][rs, :] for g, w in enumerate(weights))
        o = num * (1.0 / sum(weights))
        o_ref[rs, :] = (o * _silu(z_ref[rs, :].astype(F32))).astype(o_ref.dtype)


def _dilated_mixer(qk, vz, batch, seq):
    m = qk.shape[0]
    n_grp = len(DIL_CONFIGS)
    hb = DIL_N_HEADS
    in_specs, args = [], []
    for g in range(n_grp):
        in_specs += [pl.BlockSpec((seq, LANES), lambda b, h, g=g: (b, 2 * hb * g + h)),
                     pl.BlockSpec((seq, LANES), lambda b, h, g=g: (b, 2 * hb * g + hb + h)),
                     pl.BlockSpec((seq, LANES), lambda b, h, g=g: (b, hb * g + h))]
        args += [qk, qk, vz]
    in_specs.append(pl.BlockSpec((seq, LANES), lambda b, h: (b, hb * n_grp + h)))
    args.append(vz)
    return pl.pallas_call(
        functools.partial(_dilated_kernel, seq=seq),
        grid=(batch, DIL_N_HEADS),
        in_specs=in_specs,
        out_specs=pl.BlockSpec((seq, LANES), lambda b, h: (b, h)),
        out_shape=jax.ShapeDtypeStruct((m, DIL_WIDTH), BF16),
        scratch_shapes=[pltpu.VMEM((seq, LANES), F32)] * (3 + 2 * n_grp),
        compiler_params=_params(2),
        name="dilated_mixer",
    )(*args)


def _ssd_layer(u, h, next_norm_w, final, in_w, conv_w, conv_b, dt_bias, a_log, d_skip, norm_w, out_w,
               batch, seq):
    w_main = in_w[:, :SSM_MAIN_COLS].astype(BF16)
    w_dt = in_w[:, SSM_MAIN_COLS:].reshape(D_MODEL, SSM_N_GROUPS, SSM_HEADS_PER_GROUP)
    w_dt = jnp.zeros((D_MODEL, SSM_N_GROUPS, LANES), F32).at[:, :, :SSM_HEADS_PER_GROUP].set(w_dt)
    w_dt = w_dt.reshape(D_MODEL, SSM_N_GROUPS * LANES).astype(BF16)
    proj = _mm(u, w_main, BF16, PROJ_ROW_TILE, PROJ_COL_TILE, "ssd_in_proj")
    dt_raw = _mm(u, w_dt, F32, PROJ_ROW_TILE // 2, PROJ_COL_TILE, "ssd_dt_proj")
    y = _ssd_mixer(proj, dt_raw, conv_w, conv_b, dt_bias, a_log, d_skip, norm_w, batch, seq)
    return _out_proj(y, out_w.astype(BF16), h, next_norm_w, final)


def _mla_layer(u, h, next_norm_w, final, in_w, q_norm_w, kv_norm_w, uq_w, ukv_w, out_w, batch, seq):
    o1 = MLA_Q_LORA
    o2 = o1 + MLA_KV_LORA
    o3 = o2 + MLA_ROPE_DIM
    rope_perm = _rope_split_perm(MLA_ROPE_DIM)
    w_kr = jnp.pad(in_w[:, o2:o3], ((0, 0), (0, LANES - MLA_ROPE_DIM)))[:, rope_perm]
    w_small = jnp.concatenate([in_w[:, :o1], w_kr, in_w[:, o1:o2]], axis=1).astype(BF16)
    w_z = in_w[:, o3:].astype(BF16)
    q_scale = MLA_QK_DIM ** -0.5 * LOG2E
    w_q = (uq_w * q_scale).reshape(MLA_Q_LORA, MLA_N_HEADS, MLA_QK_DIM)
    w_q_rope = jnp.pad(w_q[:, :, MLA_NOPE_DIM:], ((0, 0), (0, 0), (0, LANES - MLA_ROPE_DIM)))[:, :, rope_perm]
    w_q = jnp.concatenate([w_q[:, :, :MLA_NOPE_DIM], w_q_rope], axis=2).reshape(MLA_Q_LORA, -1).astype(BF16)
    w_kv = ukv_w.reshape(MLA_KV_LORA, MLA_N_HEADS, 2, MLA_NOPE_DIM).transpose(0, 2, 1, 3)
    w_kv = w_kv.reshape(MLA_KV_LORA, -1).astype(BF16)
    tables = _rope_tables(seq, MLA_ROPE_DIM)

    small, k_rope = _mla_small(u, w_small, tables, seq)
    z = _mm(u, w_z, BF16, PROJ_ROW_TILE, PROJ_COL_TILE, "mla_gate_proj")
    q = _norm_mm(small, 0, MLA_Q_LORA, q_norm_w, w_q, BF16, PROJ_ROW_TILE // 2, 2 * PROJ_COL_TILE, "mla_q_proj",
                 rope_tables=tables, seq=seq)
    kv = _norm_mm(small, (MLA_SMALL_COLS - MLA_KV_LORA) // MLA_KV_LORA, MLA_KV_LORA, kv_norm_w, w_kv, BF16,
                  PROJ_ROW_TILE // 2, 2 * PROJ_COL_TILE, "mla_kv_proj")
    o = _flash(q, MLA_Q_PAD, 0, kv, 0, kv, MLA_N_HEADS, z, 0, batch, seq, MLA_N_HEADS, 512, "mla_attention",
               kr_arr=k_rope)
    return _out_proj(o, out_w.astype(BF16), h, next_norm_w, final)


def _fox_layer(u, h, next_norm_w, final, in_w, f_bias, out_w, batch, seq):
    w = FOX_WIDTH
    q_scale = FOX_HEAD_DIM ** -0.5 * LOG2E
    w_main = jnp.concatenate([in_w[:, :w] * q_scale, in_w[:, w:3 * w], in_w[:, 3 * w + FOX_N_HEADS:]],
                             axis=1).astype(BF16)
    w_f = jnp.pad(in_w[:, 3 * w:3 * w + FOX_N_HEADS], ((0, 0), (0, LANES - FOX_N_HEADS))).astype(BF16)
    f_bias_pad = jnp.pad(f_bias.astype(F32), (0, LANES - FOX_N_HEADS)).reshape(1, LANES)
    proj = _mm(u, w_main, BF16, PROJ_ROW_TILE, PROJ_COL_TILE, "fox_in_proj")
    f_raw = _mm(u, w_f, F32, PROJ_ROW_TILE, LANES, "fox_forget_proj")
    cum_t = _forget_cum(f_raw, f_bias_pad, batch, seq)
    hh = FOX_N_HEADS
    o = _flash(proj, LANES, 0, proj, hh, proj, 2 * hh, proj, 3 * hh, batch, seq, hh, 512, "fox_attention",
               cum_t=cum_t)
    return _out_proj(o, out_w.astype(BF16), h, next_norm_w, final)


def _dilated_layer(u, h, next_norm_w, final, in_w, out_w, batch, seq):
    w = DIL_WIDTH
    q_scale = DIL_HEAD_DIM ** -0.5 * LOG2E
    rope_perm = _rope_split_perm(DIL_ROPE_DIM)

    def split_rotary(cols):
        return cols.reshape(D_MODEL, DIL_N_HEADS, DIL_HEAD_DIM)[:, :, rope_perm].reshape(D_MODEL, w)

    qk_cols, v_cols = [], []
    for gi in range(len(DIL_CONFIGS)):
        base = 3 * w * gi
        qk_cols += [split_rotary(in_w[:, base:base + w] * q_scale), split_rotary(in_w[:, base + w:base + 2 * w])]
        v_cols.append(in_w[:, base + 2 * w:base + 3 * w])
    w_qk = jnp.concatenate(qk_cols, axis=1).astype(BF16)
    w_vz = jnp.concatenate(v_cols + [in_w[:, 3 * w * len(DIL_CONFIGS):]], axis=1).astype(BF16)
    tables = _rope_tables(seq, DIL_ROPE_DIM)
    qk = _mm_rope(u, w_qk, tables, seq, BF16, PROJ_ROW_TILE, PROJ_COL_TILE, "dilated_qk_proj")
    vz = _mm(u, w_vz, BF16, PROJ_ROW_TILE, PROJ_COL_TILE, "dilated_vz_proj")
    o = _dilated_mixer(qk, vz, batch, seq)
    return _out_proj(o, out_w.astype(BF16), h, next_norm_w, final)


def kernel(x, norm_w, final_norm_w, ssm_in_w, ssm_conv_w, ssm_conv_b, ssm_dt_bias, ssm_A_log, ssm_D,
           ssm_norm_w, ssm_out_w, mla_in_w, mla_q_norm_w, mla_kv_norm_w, mla_uq_w, mla_ukv_w, mla_out_w,
           fox_in_w, fox_f_bias, fox_out_w, dil_in_w, dil_out_w):
    batch, seq, d = x.shape
    depth = norm_w.shape[0]
    h = x.reshape(batch * seq, d)
    u = _rmsnorm(h, norm_w[0], BF16)
    for i in range(depth):
        kind, j = i % 4, i // 4
        final = i == depth - 1
        next_w = final_norm_w if final else norm_w[i + 1]
        if kind == 0:
            res = _ssd_layer(u, h, next_w, final, ssm_in_w[j], ssm_conv_w[j], ssm_conv_b[j], ssm_dt_bias[j],
                             ssm_A_log[j], ssm_D[j], ssm_norm_w[j], ssm_out_w[j], batch, seq)
        elif kind == 1:
            res = _mla_layer(u, h, next_w, final, mla_in_w[j], mla_q_norm_w[j], mla_kv_norm_w[j], mla_uq_w[j],
                             mla_ukv_w[j], mla_out_w[j], batch, seq)
        elif kind == 2:
            res = _fox_layer(u, h, next_w, final, fox_in_w[j], fox_f_bias[j], fox_out_w[j], batch, seq)
        else:
            res = _dilated_layer(u, h, next_w, final, dil_in_w[j], dil_out_w[j], batch, seq)
        if final:
            return res.reshape(batch, seq, d)
        h, u = res
```

```python
import functools
import math

import jax
import jax.numpy as jnp
from jax import lax
from jax.experimental import pallas as pl
from jax.experimental.pallas import tpu as pltpu

F32 = jnp.float32
BF16 = jnp.bfloat16

D_MODEL = 1024
RMS_EPS = 1e-6
ROPE_THETA = 500000.0
LOG2E = 1.4426950408889634
NEG_BIG = -1e30

LANES = 128

SSM_D_INNER = 2048
SSM_HEAD_DIM = 64
SSM_N_HEADS = 32
SSM_D_STATE = 128
SSM_N_GROUPS = 8
SSM_HEADS_PER_GROUP = SSM_N_HEADS // SSM_N_GROUPS
SSM_GROUP_WIDTH = SSM_HEADS_PER_GROUP * SSM_HEAD_DIM
SSM_CHUNK = 128
SSM_CONV = 4
SSM_BC_DIM = SSM_N_GROUPS * SSM_D_STATE
SSM_MAIN_COLS = 2 * SSM_D_INNER + 2 * SSM_BC_DIM
CONV_HALO = 16

MLA_N_HEADS = 16
MLA_Q_LORA = 384
MLA_KV_LORA = 256
MLA_NOPE_DIM = 128
MLA_ROPE_DIM = 64
MLA_V_DIM = 128
MLA_QK_DIM = MLA_NOPE_DIM + MLA_ROPE_DIM
MLA_WIDTH = MLA_N_HEADS * MLA_V_DIM
MLA_Q_PAD = 2 * LANES
MLA_SMALL_COLS = 768

FOX_N_HEADS = 16
FOX_HEAD_DIM = 128
FOX_WIDTH = FOX_N_HEADS * FOX_HEAD_DIM

DIL_CONFIGS = ((128, 1), (512, 4), (2048, 16))
DIL_N_HEADS = 8
DIL_HEAD_DIM = 128
DIL_WIDTH = DIL_N_HEADS * DIL_HEAD_DIM
DIL_ROPE_DIM = DIL_HEAD_DIM // 4

VMEM_LIMIT_BYTES = 56 * 1024 * 1024
PROJ_ROW_TILE = 2048
PROJ_COL_TILE = 1024
OUT_PROJ_ROW_TILE = 1024


def _params(n_axes):
    return pltpu.CompilerParams(dimension_semantics=("arbitrary",) * n_axes,
                                vmem_limit_bytes=VMEM_LIMIT_BYTES)


def _silu(x):
    return x * jax.nn.sigmoid(x)


def _rms_normalize(x, w):
    ms = jnp.mean(x * x, axis=-1, keepdims=True)
    return (x * lax.rsqrt(ms + RMS_EPS)) * w


ROPE_PARTNER_SHIFT = LANES // 2


def _rope_block(x, cos_t, sin_t):
    return x * cos_t + pltpu.roll(x, ROPE_PARTNER_SHIFT, axis=1) * sin_t


def _rope_split_perm(rope_dim, width=LANES):
    half = rope_dim // 2
    first = list(range(half))
    second = list(range(half, rope_dim))
    rest = list(range(rope_dim, width))
    n_fill = ROPE_PARTNER_SHIFT - half
    return jnp.array(first + rest[:n_fill] + second + rest[n_fill:], jnp.int32)


def _rope_tables(seq, rope_dim):
    half = rope_dim // 2
    inv_freq = ROPE_THETA ** (-jnp.arange(half, dtype=F32) / half)
    ang = jnp.arange(seq, dtype=F32)[:, None] * inv_freq[None, :]
    cos, sin = jnp.cos(ang), jnp.sin(ang)
    ones = jnp.ones((seq, ROPE_PARTNER_SHIFT - half), F32)
    zeros = jnp.zeros((seq, ROPE_PARTNER_SHIFT - half), F32)
    cos_t = jnp.concatenate([cos, ones, cos, ones], axis=1)
    sin_t = jnp.concatenate([-sin, zeros, sin, zeros], axis=1)
    return cos_t, sin_t


def _rmsnorm_kernel(x_ref, w_ref, o_ref):
    o_ref[...] = _rms_normalize(x_ref[...], w_ref[...]).astype(o_ref.dtype)


def _rmsnorm(x, w, out_dtype, tm=1024):
    m, d = x.shape
    tm = min(tm, m)
    return pl.pallas_call(
        _rmsnorm_kernel,
        grid=(m // tm,),
        in_specs=[pl.BlockSpec((tm, d), lambda i: (i, 0)), pl.BlockSpec((1, d), lambda i: (0, 0))],
        out_specs=pl.BlockSpec((tm, d), lambda i: (i, 0)),
        out_shape=jax.ShapeDtypeStruct((m, d), out_dtype),
        compiler_params=_params(1),
        name="rmsnorm",
    )(x, w.reshape(1, d))


def _mm_kernel(a_ref, w_ref, o_ref):
    o_ref[...] = jnp.dot(a_ref[...], w_ref[...], preferred_element_type=F32).astype(o_ref.dtype)


def _mm(a, w, out_dtype, tm, tn, name):
    m, k = a.shape
    n = w.shape[1]
    tm = min(tm, m)
    tn = min(tn, n)
    return pl.pallas_call(
        _mm_kernel,
        grid=(m // tm, n // tn),
        in_specs=[pl.BlockSpec((tm, k), lambda i, j: (i, 0)), pl.BlockSpec((k, tn), lambda i, j: (0, j))],
        out_specs=pl.BlockSpec((tm, tn), lambda i, j: (i, j)),
        out_shape=jax.ShapeDtypeStruct((m, n), out_dtype),
        compiler_params=_params(2),
        name=name,
    )(a, w)


def _mm_rope_kernel(a_ref, w_ref, cos_ref, sin_ref, o_ref):
    acc = jnp.dot(a_ref[...], w_ref[...], preferred_element_type=F32)
    cos_t, sin_t = cos_ref[...], sin_ref[...]
    for blk in range(acc.shape[1] // LANES):
        cols = slice(blk * LANES, (blk + 1) * LANES)
        o_ref[:, cols] = _rope_block(acc[:, cols], cos_t, sin_t).astype(o_ref.dtype)


def _mm_rope(a, w, tables, seq, out_dtype, tm, tn, name):
    m, k = a.shape
    n = w.shape[1]
    tm = min(tm, seq)
    row_blocks = seq // tm
    tab_spec = pl.BlockSpec((tm, LANES), lambda i, j: (i % row_blocks, 0))
    return pl.pallas_call(
        _mm_rope_kernel,
        grid=(m // tm, n // tn),
        in_specs=[pl.BlockSpec((tm, k), lambda i, j: (i, 0)), pl.BlockSpec((k, tn), lambda i, j: (0, j)),
                  tab_spec, tab_spec],
        out_specs=pl.BlockSpec((tm, tn), lambda i, j: (i, j)),
        out_shape=jax.ShapeDtypeStruct((m, n), out_dtype),
        compiler_params=_params(2),
        name=name,
    )(a, w, *tables)


def _mla_small_kernel(a_ref, w_ref, cos_ref, sin_ref, o_ref, kr_ref):
    acc = jnp.dot(a_ref[...], w_ref[...], preferred_element_type=F32)
    o_ref[...] = acc
    k_r = acc[:, MLA_Q_LORA:MLA_Q_LORA + LANES]
    kr_ref[...] = _rope_block(k_r, cos_ref[...], sin_ref[...]).astype(kr_ref.dtype)


def _mla_small(u, w, tables, seq, tm=1024):
    m, k = u.shape
    tm = min(tm, seq)
    row_blocks = seq // tm
    tab_spec = pl.BlockSpec((tm, LANES), lambda i: (i % row_blocks, 0))
    return pl.pallas_call(
        _mla_small_kernel,
        grid=(m // tm,),
        in_specs=[pl.BlockSpec((tm, k), lambda i: (i, 0)), pl.BlockSpec((k, MLA_SMALL_COLS), lambda i: (0, 0)),
                  tab_spec, tab_spec],
        out_specs=[pl.BlockSpec((tm, MLA_SMALL_COLS), lambda i: (i, 0)),
                   pl.BlockSpec((tm, LANES), lambda i: (i, 0))],
        out_shape=[jax.ShapeDtypeStruct((m, MLA_SMALL_COLS), F32), jax.ShapeDtypeStruct((m, LANES), BF16)],
        compiler_params=_params(1),
        name="mla_in_small",
    )(u, w, *tables)


def _norm_mm_kernel(a_ref, nw_ref, w_ref, *rest, rope):
    if rope:
        cos_ref, sin_ref, o_ref, an_ref = rest
    else:
        o_ref, an_ref = rest

    @pl.when(pl.program_id(1) == 0)
    def _():
        an_ref[...] = _rms_normalize(a_ref[...], nw_ref[...]).astype(an_ref.dtype)

    acc = jnp.dot(an_ref[...], w_ref[...], preferred_element_type=F32)
    if not rope:
        o_ref[...] = acc.astype(o_ref.dtype)
        return
    cos_t, sin_t = cos_ref[...], sin_ref[...]
    for blk in range(acc.shape[1] // LANES):
        cols = slice(blk * LANES, (blk + 1) * LANES)
        x = acc[:, cols]
        if blk % 2 == 1:
            x = _rope_block(x, cos_t, sin_t)
        o_ref[:, cols] = x.astype(o_ref.dtype)


def _norm_mm(a, a_col_block, k, norm_w, w, out_dtype, tm, tn, name, rope_tables=None, seq=None):
    m = a.shape[0]
    n = w.shape[1]
    tm = min(tm, m if seq is None else seq)
    in_specs = [pl.BlockSpec((tm, k), lambda i, j: (i, a_col_block)),
                pl.BlockSpec((1, k), lambda i, j: (0, 0)),
                pl.BlockSpec((k, tn), lambda i, j: (0, j))]
    args = [a, norm_w.reshape(1, k), w]
    if rope_tables is not None:
        row_blocks = seq // tm
        in_specs += [pl.BlockSpec((tm, LANES), lambda i, j: (i % row_blocks, 0))] * 2
        args += list(rope_tables)
    return pl.pallas_call(
        functools.partial(_norm_mm_kernel, rope=rope_tables is not None),
        grid=(m // tm, n // tn),
        in_specs=in_specs,
        out_specs=pl.BlockSpec((tm, tn), lambda i, j: (i, j)),
        out_shape=jax.ShapeDtypeStruct((m, n), out_dtype),
        scratch_shapes=[pltpu.VMEM((tm, k), BF16)],
        compiler_params=_params(2),
        name=name,
    )(*args)


def _out_proj_kernel(a_ref, w_ref, h_ref, nw_ref, *out_refs, final):
    h_new = h_ref[...] + jnp.dot(a_ref[...], w_ref[...], preferred_element_type=F32)
    if final:
        (u_ref,) = out_refs
    else:
        hn_ref, u_ref = out_refs
        hn_ref[...] = h_new
    u_ref[...] = _rms_normalize(h_new, nw_ref[...]).astype(u_ref.dtype)


def _out_proj(a, w, h, norm_w, final, tm=OUT_PROJ_ROW_TILE):
    m, k = a.shape
    d = w.shape[1]
    tm = min(tm, m)
    row_spec = pl.BlockSpec((tm, d), lambda i: (i, 0))
    if final:
        out_specs, out_shape = row_spec, jax.ShapeDtypeStruct((m, d), F32)
    else:
        out_specs = [row_spec, row_spec]
        out_shape = [jax.ShapeDtypeStruct((m, d), F32), jax.ShapeDtypeStruct((m, d), BF16)]
    return pl.pallas_call(
        functools.partial(_out_proj_kernel, final=final),
        grid=(m // tm,),
        in_specs=[pl.BlockSpec((tm, k), lambda i: (i, 0)), pl.BlockSpec((k, d), lambda i: (0, 0)),
                  row_spec, pl.BlockSpec((1, d), lambda i: (0, 0))],
        out_specs=out_specs,
        out_shape=out_shape,
        compiler_params=_params(1),
        name="out_proj_final" if final else "out_proj",
    )(a, w, h, norm_w.reshape(1, d))


def _expand_heads(v):
    t = v.shape[0]
    cols = [jnp.broadcast_to(v[:, h:h + 1], (t, LANES)) for h in range(SSM_HEADS_PER_GROUP)]
    lane = lax.broadcasted_iota(jnp.int32, (t, LANES), 1)
    first = lane < SSM_HEAD_DIM
    return jnp.concatenate([jnp.where(first, cols[0], cols[1]), jnp.where(first, cols[2], cols[3])], axis=1)


def _cumsum_rows(x):
    n = x.shape[0]
    row = lax.broadcasted_iota(jnp.int32, x.shape, 0)
    shift = 1
    while shift < n:
        x = x + jnp.where(row >= shift, pltpu.roll(x, shift, axis=0), 0.0)
        shift *= 2
    return x


def _ssd_kernel(z_ref, x_ref, b_ref, c_ref, dt_ref, wx_ref, wb_ref, wc_ref, bx_ref, bb_ref, bc_ref,
                dtb_ref, alog_ref, dskip_ref, nw_ref, o_ref, *, seq):
    t = SSM_CHUNK
    n_chunks = seq // t

    gw, ns = SSM_GROUP_WIDTH, SSM_D_STATE
    conv_w = jnp.concatenate([wx_ref[...], wb_ref[...], wc_ref[...]], axis=1)
    conv_b = jnp.concatenate([bx_ref[...], bb_ref[...], bc_ref[...]], axis=1)
    ext_rows = 2 * t
    n_shift = SSM_CONV - 1
    sel_row = lax.broadcasted_iota(jnp.int32, (n_shift * t, ext_rows), 0)
    sel_col = lax.broadcasted_iota(jnp.int32, (n_shift * t, ext_rows), 1)
    shift_select = (sel_col == CONV_HALO + sel_row % t - (n_shift - sel_row // t)).astype(BF16)
    ext_fill = jnp.zeros((ext_rows - t - CONV_HALO, gw + 2 * ns), BF16)

    def conv_silu(t0, c):
        rows = pl.ds(t0, t)
        cur = jnp.concatenate([x_ref[rows, :], b_ref[rows, :], c_ref[rows, :]], axis=1)
        halo = pl.ds(pl.multiple_of(jnp.maximum(t0 - CONV_HALO, 0), CONV_HALO), CONV_HALO)
        prev = jnp.concatenate([x_ref[halo, :], b_ref[halo, :], c_ref[halo, :]], axis=1)
        prev = jnp.where(c > 0, prev, jnp.zeros_like(prev))
        ext = jnp.concatenate([prev, cur, ext_fill], axis=0)
        shifted = jnp.dot(shift_select, ext, preferred_element_type=F32)
        acc = conv_b + conv_w[n_shift:n_shift + 1, :] * cur.astype(F32)
        for tap in range(n_shift):
            acc = acc + conv_w[tap:tap + 1, :] * shifted[tap * t:(tap + 1) * t, :]
        act = _silu(acc)
        return act[:, :gw], act[:, gw:gw + ns], act[:, gw + ns:]

    row = lax.broadcasted_iota(jnp.int32, (t, t), 0)
    col = lax.broadcasted_iota(jnp.int32, (t, t), 1)
    causal = row >= col
    head_of_lane = lax.broadcasted_iota(jnp.int32, (t, SSM_GROUP_WIDTH), 1) // SSM_HEAD_DIM
    neg_a = -jnp.exp(alog_ref[...])
    group_lane_shift = (LANES - SSM_HEADS_PER_GROUP * pl.program_id(1)) % LANES

    def chunk(c, state):
        t0 = pl.multiple_of(c * t, t)
        x, b_in, c_out = conv_silu(t0, c)
        dt_raw = pltpu.roll(dt_ref[pl.ds(t0, t), :], group_lane_shift, axis=1)
        dt = jax.nn.softplus(dt_raw + dtb_ref[...])
        a_cum = _cumsum_rows(dt * neg_a)
        a_cum_t = a_cum.T
        a_cum_wide = _expand_heads(a_cum)
        exp_cum = jnp.exp(a_cum_wide)
        decay_to_end = jnp.exp(a_cum_wide[t - 1:t, :] - a_cum_wide)
        xdt = x * _expand_heads(dt)

        b_bf = b_in.astype(BF16)
        c_bf = c_out.astype(BF16)
        cb = lax.dot_general(c_bf, b_bf, (((1,), (1,)), ((), ())), preferred_element_type=F32)
        y = jnp.dot(c_bf, state.astype(BF16), preferred_element_type=F32) * exp_cum
        for h in range(SSM_HEADS_PER_GROUP):
            seg = jnp.broadcast_to(a_cum[:, h:h + 1], (t, t)) - jnp.broadcast_to(a_cum_t[h:h + 1, :], (t, t))
            decay = jnp.exp(jnp.where(causal, seg, NEG_BIG))
            x_h = jnp.where(head_of_lane == h, xdt, 0.0).astype(BF16)
            y = y + jnp.dot((cb * decay).astype(BF16), x_h, preferred_element_type=F32)
        new_state = state * exp_cum[t - 1:t, :] + jnp.dot(
            b_in.T.astype(BF16), (xdt * decay_to_end).astype(BF16), preferred_element_type=F32)

        y = y + dskip_ref[...] * x
        g = y * _silu(z_ref[pl.ds(t0, t), :].astype(F32))
        o_ref[pl.ds(t0, t), :] = _rms_normalize(g, nw_ref[...]).astype(o_ref.dtype)
        return new_state

    lax.fori_loop(0, n_chunks, chunk, jnp.zeros((SSM_D_STATE, SSM_GROUP_WIDTH), F32),
                  unroll=4 if n_chunks % 4 == 0 else 1)


def _ssd_mixer(proj, dt_raw, conv_w, conv_b, dt_bias, a_log, d_skip, norm_w, batch, seq):
    m = proj.shape[0]
    gw, ns = SSM_GROUP_WIDTH, SSM_D_STATE
    x_blk0 = SSM_D_INNER // gw
    b_blk0 = 2 * SSM_D_INNER // ns
    c_blk0 = b_blk0 + SSM_N_GROUPS
    cw_b0 = SSM_D_INNER // ns
    cw_c0 = cw_b0 + SSM_N_GROUPS

    def pad_heads(v):
        out = jnp.zeros((SSM_N_GROUPS, 1, LANES), F32)
        return out.at[:, 0, :SSM_HEADS_PER_GROUP].set(v.astype(F32).reshape(SSM_N_GROUPS, SSM_HEADS_PER_GROUP))

    d_lanes = jnp.repeat(d_skip.astype(F32).reshape(SSM_N_GROUPS, SSM_HEADS_PER_GROUP), SSM_HEAD_DIM,
                         axis=1).reshape(SSM_N_GROUPS, 1, gw)
    conv_b2 = conv_b.reshape(1, -1)
    nw2 = norm_w.reshape(1, -1)
    seq_spec = lambda width, blk0: pl.BlockSpec((seq, width), lambda b, g: (b, blk0 + g))
    row_spec = lambda rows, width, blk0: pl.BlockSpec((rows, width), lambda b, g: (0, blk0 + g))
    grp_spec = lambda width: pl.BlockSpec((None, 1, width), lambda b, g: (g, 0, 0))
    return pl.pallas_call(
        functools.partial(_ssd_kernel, seq=seq),
        grid=(batch, SSM_N_GROUPS),
        in_specs=[seq_spec(gw, 0), seq_spec(gw, x_blk0), seq_spec(ns, b_blk0), seq_spec(ns, c_blk0),
                  pl.BlockSpec((seq, LANES), lambda b, g: (b, 0)),
                  row_spec(SSM_CONV, gw, 0), row_spec(SSM_CONV, ns, cw_b0), row_spec(SSM_CONV, ns, cw_c0),
                  row_spec(1, gw, 0), row_spec(1, ns, cw_b0), row_spec(1, ns, cw_c0),
                  grp_spec(LANES), grp_spec(LANES), grp_spec(gw), row_spec(1, gw, 0)],
        out_specs=pl.BlockSpec((seq, gw), lambda b, g: (b, g)),
        out_shape=jax.ShapeDtypeStruct((m, SSM_D_INNER), BF16),
        compiler_params=_params(2),
        name="ssd_mixer",
    )(proj, proj, proj, proj, dt_raw, conv_w, conv_w, conv_w, conv_b2, conv_b2, conv_b2,
      pad_heads(dt_bias), pad_heads(a_log), d_lanes, nw2)


def _flash_kernel(*refs, tile, shared_key, forget, heads_per_cum_block):
    refs = list(refs)
    q_ref, k_ref = refs[:2]
    pos = 2
    kr_ref = ck_ref = None
    if shared_key:
        kr_ref = refs[pos]
        pos += 1
    v_ref = refs[pos]
    pos += 1
    if forget:
        ck_ref = refs[pos]
        pos += 1
    z_ref, o_ref = refs[pos], refs[pos + 1]

    if forget:
        cum_row = pl.program_id(1) % heads_per_cum_block
    n_tiles = q_ref.shape[0] // tile
    dv = v_ref.shape[1]
    row = lax.broadcasted_iota(jnp.int32, (tile, tile), 0)
    col = lax.broadcasted_iota(jnp.int32, (tile, tile), 1)
    on_or_below_diagonal = row >= col

    def logits(q, k_rows):
        k = k_ref[k_rows, :]
        if shared_key:
            k = jnp.concatenate([k, kr_ref[k_rows, :]], axis=1)
        s = lax.dot_general(q, k, (((1,), (1,)), ((), ())), preferred_element_type=F32)
        if forget:
            s = s - ck_ref[pl.ds(cum_row, 1), k_rows]
        return s

    for qi in reversed(range(n_tiles)):
        q_rows = slice(qi * tile, (qi + 1) * tile)
        before = slice(0, qi * tile)
        q = q_ref[q_rows, :]
        s_diag = jnp.where(on_or_below_diagonal, logits(q, q_rows), NEG_BIG)
        m = jnp.max(s_diag, axis=1, keepdims=True)
        if qi > 0:
            s_before = logits(q, before)
            m = jnp.maximum(m, jnp.max(s_before, axis=1, keepdims=True))
        p_diag = jnp.exp2(s_diag - m)
        l = jnp.sum(p_diag, axis=1, keepdims=True)
        acc = jnp.dot(p_diag.astype(BF16), v_ref[q_rows, :], preferred_element_type=F32)
        if qi > 0:
            p_before = jnp.exp2(s_before - m)
            l = l + jnp.sum(p_before, axis=1, keepdims=True)
            acc = acc + jnp.dot(p_before.astype(BF16), v_ref[before, :], preferred_element_type=F32)
        o = acc * (1.0 / l)
        o_ref[q_rows, :] = (o * _silu(z_ref[q_rows, :].astype(F32))).astype(o_ref.dtype)


def _flash(q_arr, q_width, q_blk0, k_arr, k_blk0, v_arr, v_blk0, z_arr, z_blk0, batch, seq, heads,
           tile, name, kr_arr=None, cum_t=None):
    m = q_arr.shape[0]
    dv = LANES
    in_specs = [pl.BlockSpec((seq, q_width), lambda b, h: (b, q_blk0 + h)),
                pl.BlockSpec((seq, LANES), lambda b, h: (b, k_blk0 + h))]
    args = [q_arr, k_arr]
    if kr_arr is not None:
        in_specs.append(pl.BlockSpec((seq, LANES), lambda b, h: (b, 0)))
        args.append(kr_arr)
    in_specs.append(pl.BlockSpec((seq, dv), lambda b, h: (b, v_blk0 + h)))
    args.append(v_arr)
    cum_rows = 8
    if cum_t is not None:
        in_specs.append(pl.BlockSpec((None, cum_rows, seq), lambda b, h: (b, h // cum_rows, 0)))
        args.append(cum_t)
    in_specs.append(pl.BlockSpec((seq, dv), lambda b, h: (b, z_blk0 + h)))
    args.append(z_arr)
    return pl.pallas_call(
        functools.partial(_flash_kernel, tile=min(tile, seq), shared_key=kr_arr is not None,
                          forget=cum_t is not None, heads_per_cum_block=cum_rows),
        grid=(batch, heads),
        in_specs=in_specs,
        out_specs=pl.BlockSpec((seq, dv), lambda b, h: (b, h)),
        out_shape=jax.ShapeDtypeStruct((m, heads * dv), BF16),
        compiler_params=_params(2),
        name=name,
    )(*args)


def _forget_cum_kernel(f_ref, b_ref, o_ref):
    x = f_ref[...] + b_ref[...]
    log_f = jnp.minimum(x, 0.0) - jnp.log1p(jnp.exp(-jnp.abs(x)))
    o_ref[...] = (_cumsum_rows(log_f) * LOG2E).T


def _forget_cum(f_raw, f_bias_pad, batch, seq):
    return pl.pallas_call(
        _forget_cum_kernel,
        grid=(batch,),
        in_specs=[pl.BlockSpec((seq, LANES), lambda b: (b, 0)), pl.BlockSpec((1, LANES), lambda b: (0, 0))],
        out_specs=pl.BlockSpec((None, LANES, seq), lambda b: (b, 0, 0)),
        out_shape=jax.ShapeDtypeStruct((batch, LANES, seq), F32),
        compiler_params=_params(1),
        name="forget_cum",
    )(f_raw, f_bias_pad)


def _dilated_kernel(*refs, seq, pre_regrouped):
    n_grp = len(DIL_CONFIGS)
    qkv_refs = refs[:3 * n_grp]
    z_ref, o_ref = refs[3 * n_grp], refs[3 * n_grp + 1]
    scratch = list(refs[3 * n_grp + 2:])
    take = lambda count: [scratch.pop(0) for _ in range(count)]
    out_s = take(n_grp)
    lse_s = take(n_grp)
    stage_s = take(3)
    strided = [g for g, (_, dil) in enumerate(DIL_CONFIGS) if dil > 1]
    regroup_here = [g for g in strided if g not in pre_regrouped]
    qkv_res = {g: take(3) for g in regroup_here}
    for g in pre_regrouped:
        qkv_res[g] = qkv_refs[3 * g:3 * g + 3]
    out_res = {g: take(1)[0] for g in strided}
    lse_res = {g: take(1)[0] for g in strided}

    n_copy = 0
    for g in regroup_here:
        dil = DIL_CONFIGS[g][1]
        length = seq // dil
        for which in range(3):
            stage = stage_s[n_copy % len(stage_s)]
            n_copy += 1
            stage[...] = qkv_refs[3 * g + which][...].astype(F32)
            for residue in range(dil):
                qkv_res[g][which][residue * length:(residue + 1) * length, :] = (
                    stage[pl.ds(residue, length, stride=dil), :].astype(BF16))

    for g, (window, dil) in enumerate(DIL_CONFIGS):
        span = window // dil
        length = seq // dil
        blk = min(span, length)
        n_blk = length // blk
        q_src, k_src, v_src = qkv_res[g] if dil > 1 else qkv_refs[3 * g:3 * g + 3]
        o_dst, l_dst = (out_res[g], lse_res[g]) if dil > 1 else (out_s[g], lse_s[g])
        row1 = lax.broadcasted_iota(jnp.int32, (blk, blk), 0)
        col1 = lax.broadcasted_iota(jnp.int32, (blk, blk), 1)
        band_first = (row1 >= col1) & (row1 - col1 <= span)
        row2 = lax.broadcasted_iota(jnp.int32, (blk, 2 * blk), 0)
        col2 = lax.broadcasted_iota(jnp.int32, (blk, 2 * blk), 1)
        dist = row2 + blk - col2
        band = (dist >= 0) & (dist <= span)
        for residue in range(dil):
            for n in range(n_blk):
                rows = slice(residue * length + n * blk, residue * length + (n + 1) * blk)
                keys = rows if n == 0 else slice(rows.start - blk, rows.stop)
                s = lax.dot_general(q_src[rows, :], k_src[keys, :], (((1,), (1,)), ((), ())),
                                    preferred_element_type=F32)
                s = jnp.where(band_first if n == 0 else band, s, NEG_BIG)
                m = jnp.max(s, axis=1, keepdims=True)
                p = jnp.exp2(s - m)
                den = jnp.sum(p, axis=1, keepdims=True)
                o_dst[rows, :] = jnp.dot(p.astype(BF16), v_src[keys, :], preferred_element_type=F32) * (1.0 / den)
                l_dst[rows, :] = jnp.broadcast_to(m + jnp.log2(den), (blk, LANES))

    for g in strided:
        dil = DIL_CONFIGS[g][1]
        length = seq // dil
        for residue in range(dil):
            src_rows = slice(residue * length, (residue + 1) * length)
            out_s[g][pl.ds(residue, length, stride=dil), :] = out_res[g][src_rows, :]
            lse_s[g][pl.ds(residue, length, stride=dil), :] = lse_res[g][src_rows, :]

    rows_per_step = min(256, seq)
    for i in range(seq // rows_per_step):
        rs = slice(i * rows_per_step, (i + 1) * rows_per_step)
        lses = [lse_s[g][rs, :] for g in range(n_grp)]
        lse_max = functools.reduce(jnp.maximum, lses)
        weights = [jnp.exp2(l - lse_max) for l in lses]
        num = sum(w * out_s[g][rs, :] for g, w in enumerate(weights))
        o = num * (1.0 / sum(weights))
        o_ref[rs, :] = (o * _silu(z_ref[rs, :].astype(F32))).astype(o_ref.dtype)


def _dilated_mixer(qkv_operands, z_operand, pre_regrouped, batch, seq):
    m = z_operand[0].shape[0]
    n_grp = len(DIL_CONFIGS)
    n_strided = sum(dil > 1 for _, dil in DIL_CONFIGS)
    n_regroup = n_strided - len(pre_regrouped)
    head_block = lambda blk0: pl.BlockSpec((seq, LANES), lambda b, h: (b, blk0 + h))
    operands = list(qkv_operands) + [z_operand]
    in_specs = [head_block(blk0) for _, blk0 in operands]
    args = [arr for arr, _ in operands]
    return pl.pallas_call(
        functools.partial(_dilated_kernel, seq=seq, pre_regrouped=tuple(pre_regrouped)),
        grid=(batch, DIL_N_HEADS),
        in_specs=in_specs,
        out_specs=pl.BlockSpec((seq, LANES), lambda b, h: (b, h)),
        out_shape=jax.ShapeDtypeStruct((m, DIL_WIDTH), BF16),
        scratch_shapes=([pltpu.VMEM((seq, LANES), F32)] * (2 * n_grp + 3)
                        + [pltpu.VMEM((seq, LANES), BF16)] * 3 * n_regroup
                        + [pltpu.VMEM((seq, LANES), F32)] * 2 * n_strided),
        compiler_params=_params(2),
        name="dilated_mixer",
    )(*args)


def _ssd_layer(u, h, next_norm_w, final, in_w, conv_w, conv_b, dt_bias, a_log, d_skip, norm_w, out_w,
               batch, seq):
    w_main = in_w[:, :SSM_MAIN_COLS].astype(BF16)
    w_dt = jnp.pad(in_w[:, SSM_MAIN_COLS:], ((0, 0), (0, LANES - SSM_N_HEADS))).astype(BF16)
    proj = _mm(u, w_main, BF16, PROJ_ROW_TILE, PROJ_COL_TILE, "ssd_in_proj")
    dt_raw = _mm(u, w_dt, F32, PROJ_ROW_TILE, PROJ_COL_TILE, "ssd_dt_proj")
    y = _ssd_mixer(proj, dt_raw, conv_w, conv_b, dt_bias, a_log, d_skip, norm_w, batch, seq)
    return _out_proj(y, out_w.astype(BF16), h, next_norm_w, final)


def _mla_layer(u, h, next_norm_w, final, in_w, q_norm_w, kv_norm_w, uq_w, ukv_w, out_w, batch, seq):
    o1 = MLA_Q_LORA
    o2 = o1 + MLA_KV_LORA
    o3 = o2 + MLA_ROPE_DIM
    rope_perm = _rope_split_perm(MLA_ROPE_DIM)
    w_kr = jnp.pad(in_w[:, o2:o3], ((0, 0), (0, LANES - MLA_ROPE_DIM)))[:, rope_perm]
    w_small = jnp.concatenate([in_w[:, :o1], w_kr, in_w[:, o1:o2]], axis=1).astype(BF16)
    w_z = in_w[:, o3:].astype(BF16)
    q_scale = MLA_QK_DIM ** -0.5 * LOG2E
    w_q = (uq_w * q_scale).reshape(MLA_Q_LORA, MLA_N_HEADS, MLA_QK_DIM)
    w_q_rope = jnp.pad(w_q[:, :, MLA_NOPE_DIM:], ((0, 0), (0, 0), (0, LANES - MLA_ROPE_DIM)))[:, :, rope_perm]
    w_q = jnp.concatenate([w_q[:, :, :MLA_NOPE_DIM], w_q_rope], axis=2).reshape(MLA_Q_LORA, -1).astype(BF16)
    w_kv = ukv_w.reshape(MLA_KV_LORA, MLA_N_HEADS, 2, MLA_NOPE_DIM).transpose(0, 2, 1, 3)
    w_kv = w_kv.reshape(MLA_KV_LORA, -1).astype(BF16)
    tables = _rope_tables(seq, MLA_ROPE_DIM)

    small, k_rope = _mla_small(u, w_small, tables, seq)
    z = _mm(u, w_z, BF16, PROJ_ROW_TILE, PROJ_COL_TILE, "mla_gate_proj")
    q = _norm_mm(small, 0, MLA_Q_LORA, q_norm_w, w_q, BF16, PROJ_ROW_TILE // 2, 2 * PROJ_COL_TILE, "mla_q_proj",
                 rope_tables=tables, seq=seq)
    kv = _norm_mm(small, (MLA_SMALL_COLS - MLA_KV_LORA) // MLA_KV_LORA, MLA_KV_LORA, kv_norm_w, w_kv, BF16,
                  PROJ_ROW_TILE // 2, 2 * PROJ_COL_TILE, "mla_kv_proj")
    o = _flash(q, MLA_Q_PAD, 0, kv, 0, kv, MLA_N_HEADS, z, 0, batch, seq, MLA_N_HEADS, 512, "mla_attention",
               kr_arr=k_rope)
    return _out_proj(o, out_w.astype(BF16), h, next_norm_w, final)


def _fox_layer(u, h, next_norm_w, final, in_w, f_bias, out_w, batch, seq):
    w = FOX_WIDTH
    q_scale = FOX_HEAD_DIM ** -0.5 * LOG2E
    w_main = jnp.concatenate([in_w[:, :w] * q_scale, in_w[:, w:3 * w], in_w[:, 3 * w + FOX_N_HEADS:]],
                             axis=1).astype(BF16)
    w_f = jnp.pad(in_w[:, 3 * w:3 * w + FOX_N_HEADS], ((0, 0), (0, LANES - FOX_N_HEADS))).astype(BF16)
    f_bias_pad = jnp.pad(f_bias.astype(F32), (0, LANES - FOX_N_HEADS)).reshape(1, LANES)
    proj = _mm(u, w_main, BF16, PROJ_ROW_TILE, PROJ_COL_TILE, "fox_in_proj")
    f_raw = _mm(u, w_f, F32, PROJ_ROW_TILE, LANES, "fox_forget_proj")
    cum_t = _forget_cum(f_raw, f_bias_pad, batch, seq)
    hh = FOX_N_HEADS
    o = _flash(proj, LANES, 0, proj, hh, proj, 2 * hh, proj, 3 * hh, batch, seq, hh, 512, "fox_attention",
               cum_t=cum_t)
    return _out_proj(o, out_w.astype(BF16), h, next_norm_w, final)


def _dilated_layer(u, h, next_norm_w, final, in_w, out_w, batch, seq):
    w = DIL_WIDTH
    q_scale = DIL_HEAD_DIM ** -0.5 * LOG2E
    rope_perm = _rope_split_perm(DIL_ROPE_DIM)

    def split_rotary(cols):
        return cols.reshape(D_MODEL, DIL_N_HEADS, DIL_HEAD_DIM)[:, :, rope_perm].reshape(D_MODEL, w)

    n_grp = len(DIL_CONFIGS)
    qk_cols, v_cols = [], []
    for gi in range(n_grp):
        base = 3 * w * gi
        qk_cols += [split_rotary(in_w[:, base:base + w] * q_scale), split_rotary(in_w[:, base + w:base + 2 * w])]
        v_cols.append(in_w[:, base + 2 * w:base + 3 * w])
    w_gate = in_w[:, 3 * w * n_grp:]
    tables = _rope_tables(seq, DIL_ROPE_DIM)

    last = n_grp - 1
    dil = DIL_CONFIGS[last][1]

    def residue_major(a, lead):
        cols = a.shape[-1]
        return a.reshape(lead, seq // dil, dil, cols).transpose(0, 2, 1, 3).reshape(lead * seq, cols)

    u_res = residue_major(u, batch)
    tables_res = tuple(residue_major(t, 1) for t in tables)
    w_qk = jnp.concatenate(qk_cols[:2 * last], axis=1).astype(BF16)
    w_qk_last = jnp.concatenate(qk_cols[2 * last:], axis=1).astype(BF16)
    w_vz = jnp.concatenate(v_cols[:last] + [w_gate], axis=1).astype(BF16)
    qk = _mm_rope(u, w_qk, tables, seq, BF16, PROJ_ROW_TILE, PROJ_COL_TILE, "dilated_qk_proj")
    qk_last = _mm_rope(u_res, w_qk_last, tables_res, seq, BF16, PROJ_ROW_TILE, PROJ_COL_TILE,
                       "dilated_qk_proj_regrouped")
    vz = _mm(u, w_vz, BF16, PROJ_ROW_TILE, PROJ_COL_TILE, "dilated_vz_proj")
    v_last = _mm(u_res, v_cols[last].astype(BF16), BF16, PROJ_ROW_TILE, PROJ_COL_TILE, "dilated_v_proj_regrouped")
    hb = DIL_N_HEADS
    operands = []
    for gi in range(last):
        operands += [(qk, 2 * hb * gi), (qk, 2 * hb * gi + hb), (vz, hb * gi)]
    operands += [(qk_last, 0), (qk_last, hb), (v_last, 0)]
    o = _dilated_mixer(operands, (vz, hb * last), (last,), batch, seq)
    return _out_proj(o, out_w.astype(BF16), h, next_norm_w, final)


def kernel(x, norm_w, final_norm_w, ssm_in_w, ssm_conv_w, ssm_conv_b, ssm_dt_bias, ssm_A_log, ssm_D,
           ssm_norm_w, ssm_out_w, mla_in_w, mla_q_norm_w, mla_kv_norm_w, mla_uq_w, mla_ukv_w, mla_out_w,
           fox_in_w, fox_f_bias, fox_out_w, dil_in_w, dil_out_w):
    batch, seq, d = x.shape
    depth = norm_w.shape[0]
    h = x.reshape(batch * seq, d)
    u = _rmsnorm(h, norm_w[0], BF16)
    for i in range(depth):
        kind, j = i % 4, i // 4
        final = i == depth - 1
        next_w = final_norm_w if final else norm_w[i + 1]
        if kind == 0:
            res = _ssd_layer(u, h, next_w, final, ssm_in_w[j], ssm_conv_w[j], ssm_conv_b[j], ssm_dt_bias[j],
                             ssm_A_log[j], ssm_D[j], ssm_norm_w[j], ssm_out_w[j], batch, seq)
        elif kind == 1:
            res = _mla_layer(u, h, next_w, final, mla_in_w[j], mla_q_norm_w[j], mla_kv_norm_w[j], mla_uq_w[j],
                             mla_ukv_w[j], mla_out_w[j], batch, seq)
        elif kind == 2:
            res = _fox_layer(u, h, next_w, final, fox_in_w[j], fox_f_bias[j], fox_out_w[j], batch, seq)
        else:
            res = _dilated_layer(u, h, next_w, final, dil_in_w[j], dil_out_w[j], batch, seq)
        if final:
            return res.reshape(batch, seq, d)
        h, u = res
```

```python
import functools
import math

import jax
import jax.numpy as jnp
from jax import lax
from jax.experimental import pallas as pl
from jax.experimental.pallas import tpu as pltpu

F32 = jnp.float32
BF16 = jnp.bfloat16

D_MODEL = 1024
RMS_EPS = 1e-6
ROPE_THETA = 500000.0
LOG2E = 1.4426950408889634
NEG_BIG = -1e30

LANES = 128

SSM_D_INNER = 2048
SSM_HEAD_DIM = 64
SSM_N_HEADS = 32
SSM_D_STATE = 128
SSM_N_GROUPS = 8
SSM_HEADS_PER_GROUP = SSM_N_HEADS // SSM_N_GROUPS
SSM_GROUP_WIDTH = SSM_HEADS_PER_GROUP * SSM_HEAD_DIM
SSM_CHUNK = 128
SSM_CONV = 4
SSM_BC_DIM = SSM_N_GROUPS * SSM_D_STATE
SSM_MAIN_COLS = 2 * SSM_D_INNER + 2 * SSM_BC_DIM
CONV_HALO = 16
SSD_CHUNKS_PER_STEP = 4

MLA_N_HEADS = 16
MLA_Q_LORA = 384
MLA_KV_LORA = 256
MLA_NOPE_DIM = 128
MLA_ROPE_DIM = 64
MLA_V_DIM = 128
MLA_QK_DIM = MLA_NOPE_DIM + MLA_ROPE_DIM
MLA_WIDTH = MLA_N_HEADS * MLA_V_DIM
MLA_Q_PAD = 2 * LANES
MLA_SMALL_COLS = 768

FOX_N_HEADS = 16
FOX_HEAD_DIM = 128
FOX_WIDTH = FOX_N_HEADS * FOX_HEAD_DIM

DIL_CONFIGS = ((128, 1), (512, 4), (2048, 16))
DIL_N_HEADS = 8
DIL_HEAD_DIM = 128
DIL_WIDTH = DIL_N_HEADS * DIL_HEAD_DIM
DIL_ROPE_DIM = DIL_HEAD_DIM // 4

VMEM_LIMIT_BYTES = 56 * 1024 * 1024
PROJ_ROW_TILE = 2048
PROJ_COL_TILE = 1024
OUT_PROJ_ROW_TILE = 1024


def _params(n_axes):
    return pltpu.CompilerParams(dimension_semantics=("arbitrary",) * n_axes,
                                vmem_limit_bytes=VMEM_LIMIT_BYTES)


def _silu(x):
    return x * jax.nn.sigmoid(x)


def _rms_normalize(x, w):
    ms = jnp.mean(x * x, axis=-1, keepdims=True)
    return (x * lax.rsqrt(ms + RMS_EPS)) * w


ROPE_PARTNER_SHIFT = LANES // 2


def _rope_block(x, cos_t, sin_t):
    return x * cos_t + pltpu.roll(x, ROPE_PARTNER_SHIFT, axis=1) * sin_t


def _rope_split_perm(rope_dim, width=LANES):
    half = rope_dim // 2
    first = list(range(half))
    second = list(range(half, rope_dim))
    rest = list(range(rope_dim, width))
    n_fill = ROPE_PARTNER_SHIFT - half
    return jnp.array(first + rest[:n_fill] + second + rest[n_fill:], jnp.int32)


def _rope_tables(seq, rope_dim):
    half = rope_dim // 2
    inv_freq = ROPE_THETA ** (-jnp.arange(half, dtype=F32) / half)
    ang = jnp.arange(seq, dtype=F32)[:, None] * inv_freq[None, :]
    cos, sin = jnp.cos(ang), jnp.sin(ang)
    ones = jnp.ones((seq, ROPE_PARTNER_SHIFT - half), F32)
    zeros = jnp.zeros((seq, ROPE_PARTNER_SHIFT - half), F32)
    cos_t = jnp.concatenate([cos, ones, cos, ones], axis=1)
    sin_t = jnp.concatenate([-sin, zeros, sin, zeros], axis=1)
    return cos_t, sin_t


def _rmsnorm_kernel(x_ref, w_ref, o_ref):
    o_ref[...] = _rms_normalize(x_ref[...], w_ref[...]).astype(o_ref.dtype)


def _rmsnorm(x, w, out_dtype, tm=1024):
    m, d = x.shape
    tm = min(tm, m)
    return pl.pallas_call(
        _rmsnorm_kernel,
        grid=(m // tm,),
        in_specs=[pl.BlockSpec((tm, d), lambda i: (i, 0)), pl.BlockSpec((1, d), lambda i: (0, 0))],
        out_specs=pl.BlockSpec((tm, d), lambda i: (i, 0)),
        out_shape=jax.ShapeDtypeStruct((m, d), out_dtype),
        compiler_params=_params(1),
        name="rmsnorm",
    )(x, w.reshape(1, d))


def _mm_kernel(a_ref, w_ref, o_ref):
    o_ref[...] = jnp.dot(a_ref[...], w_ref[...], preferred_element_type=F32).astype(o_ref.dtype)


def _mm(a, w, out_dtype, tm, tn, name):
    m, k = a.shape
    n = w.shape[1]
    tm = min(tm, m)
    tn = min(tn, n)
    return pl.pallas_call(
        _mm_kernel,
        grid=(m // tm, n // tn),
        in_specs=[pl.BlockSpec((tm, k), lambda i, j: (i, 0)), pl.BlockSpec((k, tn), lambda i, j: (0, j))],
        out_specs=pl.BlockSpec((tm, tn), lambda i, j: (i, j)),
        out_shape=jax.ShapeDtypeStruct((m, n), out_dtype),
        compiler_params=_params(2),
        name=name,
    )(a, w)


def _mm_rope_kernel(a_ref, w_ref, cos_ref, sin_ref, o_ref):
    acc = jnp.dot(a_ref[...], w_ref[...], preferred_element_type=F32)
    cos_t, sin_t = cos_ref[...], sin_ref[...]
    for blk in range(acc.shape[1] // LANES):
        cols = slice(blk * LANES, (blk + 1) * LANES)
        o_ref[:, cols] = _rope_block(acc[:, cols], cos_t, sin_t).astype(o_ref.dtype)


def _mm_rope(a, w, tables, seq, out_dtype, tm, tn, name):
    m, k = a.shape
    n = w.shape[1]
    tm = min(tm, seq)
    row_blocks = seq // tm
    tab_spec = pl.BlockSpec((tm, LANES), lambda i, j: (i % row_blocks, 0))
    return pl.pallas_call(
        _mm_rope_kernel,
        grid=(m // tm, n // tn),
        in_specs=[pl.BlockSpec((tm, k), lambda i, j: (i, 0)), pl.BlockSpec((k, tn), lambda i, j: (0, j)),
                  tab_spec, tab_spec],
        out_specs=pl.BlockSpec((tm, tn), lambda i, j: (i, j)),
        out_shape=jax.ShapeDtypeStruct((m, n), out_dtype),
        compiler_params=_params(2),
        name=name,
    )(a, w, *tables)


def _mla_small_kernel(a_ref, w_ref, cos_ref, sin_ref, o_ref, kr_ref):
    acc = jnp.dot(a_ref[...], w_ref[...], preferred_element_type=F32)
    o_ref[...] = acc
    k_r = acc[:, MLA_Q_LORA:MLA_Q_LORA + LANES]
    kr_ref[...] = _rope_block(k_r, cos_ref[...], sin_ref[...]).astype(kr_ref.dtype)


def _mla_small(u, w, tables, seq, tm=1024):
    m, k = u.shape
    tm = min(tm, seq)
    row_blocks = seq // tm
    tab_spec = pl.BlockSpec((tm, LANES), lambda i: (i % row_blocks, 0))
    return pl.pallas_call(
        _mla_small_kernel,
        grid=(m // tm,),
        in_specs=[pl.BlockSpec((tm, k), lambda i: (i, 0)), pl.BlockSpec((k, MLA_SMALL_COLS), lambda i: (0, 0)),
                  tab_spec, tab_spec],
        out_specs=[pl.BlockSpec((tm, MLA_SMALL_COLS), lambda i: (i, 0)),
                   pl.BlockSpec((tm, LANES), lambda i: (i, 0))],
        out_shape=[jax.ShapeDtypeStruct((m, MLA_SMALL_COLS), F32), jax.ShapeDtypeStruct((m, LANES), BF16)],
        compiler_params=_params(1),
        name="mla_in_small",
    )(u, w, *tables)


def _norm_mm_kernel(a_ref, nw_ref, w_ref, *rest, rope):
    if rope:
        cos_ref, sin_ref, o_ref, an_ref = rest
    else:
        o_ref, an_ref = rest

    @pl.when(pl.program_id(1) == 0)
    def _():
        an_ref[...] = _rms_normalize(a_ref[...], nw_ref[...]).astype(an_ref.dtype)

    acc = jnp.dot(an_ref[...], w_ref[...], preferred_element_type=F32)
    if not rope:
        o_ref[...] = acc.astype(o_ref.dtype)
        return
    cos_t, sin_t = cos_ref[...], sin_ref[...]
    for blk in range(acc.shape[1] // LANES):
        cols = slice(blk * LANES, (blk + 1) * LANES)
        x = acc[:, cols]
        if blk % 2 == 1:
            x = _rope_block(x, cos_t, sin_t)
        o_ref[:, cols] = x.astype(o_ref.dtype)


def _norm_mm(a, a_col_block, k, norm_w, w, out_dtype, tm, tn, name, rope_tables=None, seq=None):
    m = a.shape[0]
    n = w.shape[1]
    tm = min(tm, m if seq is None else seq)
    in_specs = [pl.BlockSpec((tm, k), lambda i, j: (i, a_col_block)),
                pl.BlockSpec((1, k), lambda i, j: (0, 0)),
                pl.BlockSpec((k, tn), lambda i, j: (0, j))]
    args = [a, norm_w.reshape(1, k), w]
    if rope_tables is not None:
        row_blocks = seq // tm
        in_specs += [pl.BlockSpec((tm, LANES), lambda i, j: (i % row_blocks, 0))] * 2
        args += list(rope_tables)
    return pl.pallas_call(
        functools.partial(_norm_mm_kernel, rope=rope_tables is not None),
        grid=(m // tm, n // tn),
        in_specs=in_specs,
        out_specs=pl.BlockSpec((tm, tn), lambda i, j: (i, j)),
        out_shape=jax.ShapeDtypeStruct((m, n), out_dtype),
        scratch_shapes=[pltpu.VMEM((tm, k), BF16)],
        compiler_params=_params(2),
        name=name,
    )(*args)


def _out_proj_kernel(a_ref, w_ref, h_ref, nw_ref, *out_refs, final):
    h_new = h_ref[...] + jnp.dot(a_ref[...], w_ref[...], preferred_element_type=F32)
    if final:
        (u_ref,) = out_refs
    else:
        hn_ref, u_ref = out_refs
        hn_ref[...] = h_new
    u_ref[...] = _rms_normalize(h_new, nw_ref[...]).astype(u_ref.dtype)


def _out_proj(a, w, h, norm_w, final, tm=OUT_PROJ_ROW_TILE):
    m, k = a.shape
    d = w.shape[1]
    tm = min(tm, m)
    row_spec = pl.BlockSpec((tm, d), lambda i: (i, 0))
    if final:
        out_specs, out_shape = row_spec, jax.ShapeDtypeStruct((m, d), F32)
    else:
        out_specs = [row_spec, row_spec]
        out_shape = [jax.ShapeDtypeStruct((m, d), F32), jax.ShapeDtypeStruct((m, d), BF16)]
    return pl.pallas_call(
        functools.partial(_out_proj_kernel, final=final),
        grid=(m // tm,),
        in_specs=[pl.BlockSpec((tm, k), lambda i: (i, 0)), pl.BlockSpec((k, d), lambda i: (0, 0)),
                  row_spec, pl.BlockSpec((1, d), lambda i: (0, 0))],
        out_specs=out_specs,
        out_shape=out_shape,
        compiler_params=_params(1),
        name="out_proj_final" if final else "out_proj",
    )(a, w, h, norm_w.reshape(1, d))


def _expand_heads(v):
    t = v.shape[0]
    cols = [jnp.broadcast_to(v[:, h:h + 1], (t, LANES)) for h in range(SSM_HEADS_PER_GROUP)]
    lane = lax.broadcasted_iota(jnp.int32, (t, LANES), 1)
    first = lane < SSM_HEAD_DIM
    return jnp.concatenate([jnp.where(first, cols[0], cols[1]), jnp.where(first, cols[2], cols[3])], axis=1)


def _cumsum_rows(x):
    n = x.shape[0]
    row = lax.broadcasted_iota(jnp.int32, x.shape, 0)
    shift = 1
    while shift < n:
        x = x + jnp.where(row >= shift, pltpu.roll(x, shift, axis=0), 0.0)
        shift *= 2
    return x


def _ssd_kernel(z_ref, x_ref, b_ref, c_ref, dt_ref, wx_ref, wb_ref, wc_ref, bx_ref, bb_ref, bc_ref,
                dtb_ref, alog_ref, dskip_ref, nw_ref, o_ref, *, seq):
    t = SSM_CHUNK
    n_chunks = seq // t

    gw, ns = SSM_GROUP_WIDTH, SSM_D_STATE
    conv_w = jnp.concatenate([wx_ref[...], wb_ref[...], wc_ref[...]], axis=1)
    conv_b = jnp.concatenate([bx_ref[...], bb_ref[...], bc_ref[...]], axis=1)
    ext_rows = 2 * t
    n_shift = SSM_CONV - 1
    sel_row = lax.broadcasted_iota(jnp.int32, (n_shift * t, ext_rows), 0)
    sel_col = lax.broadcasted_iota(jnp.int32, (n_shift * t, ext_rows), 1)
    shift_select = (sel_col == CONV_HALO + sel_row % t - (n_shift - sel_row // t)).astype(BF16)
    ext_fill = jnp.zeros((ext_rows - t - CONV_HALO, gw + 2 * ns), BF16)

    def conv_shifts(t0, c):
        rows = pl.ds(t0, t)
        cur = jnp.concatenate([x_ref[rows, :], b_ref[rows, :], c_ref[rows, :]], axis=1)
        halo = pl.ds(pl.multiple_of(jnp.maximum(t0 - CONV_HALO, 0), CONV_HALO), CONV_HALO)
        prev = jnp.concatenate([x_ref[halo, :], b_ref[halo, :], c_ref[halo, :]], axis=1)
        prev = jnp.where(c > 0, prev, jnp.zeros_like(prev))
        ext = jnp.concatenate([prev, cur, ext_fill], axis=0)
        return cur, jnp.dot(shift_select, ext, preferred_element_type=F32)

    def conv_silu(cur, shifted):
        acc = conv_b + conv_w[n_shift:n_shift + 1, :] * cur.astype(F32)
        for tap in range(n_shift):
            acc = acc + conv_w[tap:tap + 1, :] * shifted[tap * t:(tap + 1) * t, :]
        act = _silu(acc)
        return act[:, :gw], act[:, gw:gw + ns], act[:, gw + ns:]

    row = lax.broadcasted_iota(jnp.int32, (t, t), 0)
    col = lax.broadcasted_iota(jnp.int32, (t, t), 1)
    causal = row >= col
    head_of_lane = lax.broadcasted_iota(jnp.int32, (t, SSM_GROUP_WIDTH), 1) // SSM_HEAD_DIM
    neg_a = -jnp.exp(alog_ref[...])
    group_lane_shift = (LANES - SSM_HEADS_PER_GROUP * pl.program_id(1)) % LANES

    per_step = SSD_CHUNKS_PER_STEP if n_chunks % SSD_CHUNKS_PER_STEP == 0 else 1

    def chunk_group(i, state):
        chunks = [i * per_step + j for j in range(per_step)]
        starts = [pl.multiple_of(c * t, t) for c in chunks]
        conv_in = [conv_shifts(t0, c) for t0, c in zip(starts, chunks)]

        stage2 = []
        for t0, (cur, shifted) in zip(starts, conv_in):
            x, b_in, c_out = conv_silu(cur, shifted)
            dt_raw = pltpu.roll(dt_ref[pl.ds(t0, t), :], group_lane_shift, axis=1)
            dt = jax.nn.softplus(dt_raw + dtb_ref[...])
            a_cum = _cumsum_rows(dt * neg_a)
            a_cum_wide = _expand_heads(a_cum)
            exp_cum = jnp.exp(a_cum_wide)
            decay_to_end = jnp.exp(a_cum_wide[t - 1:t, :] - a_cum_wide)
            xdt = x * _expand_heads(dt)
            c_bf = c_out.astype(BF16)
            cb = lax.dot_general(c_bf, b_in.astype(BF16), (((1,), (1,)), ((), ())),
                                 preferred_element_type=F32)
            stage2.append((x, b_in, c_bf, a_cum, exp_cum, decay_to_end, xdt, cb))

        stage3 = []
        for x, b_in, c_bf, a_cum, exp_cum, decay_to_end, xdt, cb in stage2:
            a_cum_t = a_cum.T
            y_diag = None
            for h in range(SSM_HEADS_PER_GROUP):
                seg = jnp.broadcast_to(a_cum[:, h:h + 1], (t, t)) - jnp.broadcast_to(a_cum_t[h:h + 1, :], (t, t))
                decay = jnp.exp(jnp.where(causal, seg, NEG_BIG))
                x_h = jnp.where(head_of_lane == h, xdt, 0.0).astype(BF16)
                term = jnp.dot((cb * decay).astype(BF16), x_h, preferred_element_type=F32)
                y_diag = term if y_diag is None else y_diag + term
            stage3.append((x, c_bf, exp_cum, y_diag, b_in.T.astype(BF16), (xdt * decay_to_end).astype(BF16)))

        for t0, (x, c_bf, exp_cum, y_diag, b_t, x_to_end) in zip(starts, stage3):
            y = y_diag + jnp.dot(c_bf, state.astype(BF16), preferred_element_type=F32) * exp_cum
            state = state * exp_cum[t - 1:t, :] + jnp.dot(b_t, x_to_end, preferred_element_type=F32)
            y = y + dskip_ref[...] * x
            g = y * _silu(z_ref[pl.ds(t0, t), :].astype(F32))
            o_ref[pl.ds(t0, t), :] = _rms_normalize(g, nw_ref[...]).astype(o_ref.dtype)
        return state

    lax.fori_loop(0, n_chunks // per_step, chunk_group, jnp.zeros((SSM_D_STATE, SSM_GROUP_WIDTH), F32))


def _ssd_mixer(proj, dt_raw, conv_w, conv_b, dt_bias, a_log, d_skip, norm_w, batch, seq):
    m = proj.shape[0]
    gw, ns = SSM_GROUP_WIDTH, SSM_D_STATE
    x_blk0 = SSM_D_INNER // gw
    b_blk0 = 2 * SSM_D_INNER // ns
    c_blk0 = b_blk0 + SSM_N_GROUPS
    cw_b0 = SSM_D_INNER // ns
    cw_c0 = cw_b0 + SSM_N_GROUPS

    def pad_heads(v):
        out = jnp.zeros((SSM_N_GROUPS, 1, LANES), F32)
        return out.at[:, 0, :SSM_HEADS_PER_GROUP].set(v.astype(F32).reshape(SSM_N_GROUPS, SSM_HEADS_PER_GROUP))

    d_lanes = jnp.repeat(d_skip.astype(F32).reshape(SSM_N_GROUPS, SSM_HEADS_PER_GROUP), SSM_HEAD_DIM,
                         axis=1).reshape(SSM_N_GROUPS, 1, gw)
    conv_b2 = conv_b.reshape(1, -1)
    nw2 = norm_w.reshape(1, -1)
    seq_spec = lambda width, blk0: pl.BlockSpec((seq, width), lambda b, g: (b, blk0 + g))
    row_spec = lambda rows, width, blk0: pl.BlockSpec((rows, width), lambda b, g: (0, blk0 + g))
    grp_spec = lambda width: pl.BlockSpec((None, 1, width), lambda b, g: (g, 0, 0))
    return pl.pallas_call(
        functools.partial(_ssd_kernel, seq=seq),
        grid=(batch, SSM_N_GROUPS),
        in_specs=[seq_spec(gw, 0), seq_spec(gw, x_blk0), seq_spec(ns, b_blk0), seq_spec(ns, c_blk0),
                  pl.BlockSpec((seq, LANES), lambda b, g: (b, 0)),
                  row_spec(SSM_CONV, gw, 0), row_spec(SSM_CONV, ns, cw_b0), row_spec(SSM_CONV, ns, cw_c0),
                  row_spec(1, gw, 0), row_spec(1, ns, cw_b0), row_spec(1, ns, cw_c0),
                  grp_spec(LANES), grp_spec(LANES), grp_spec(gw), row_spec(1, gw, 0)],
        out_specs=pl.BlockSpec((seq, gw), lambda b, g: (b, g)),
        out_shape=jax.ShapeDtypeStruct((m, SSM_D_INNER), BF16),
        compiler_params=_params(2),
        name="ssd_mixer",
    )(proj, proj, proj, proj, dt_raw, conv_w, conv_w, conv_w, conv_b2, conv_b2, conv_b2,
      pad_heads(dt_bias), pad_heads(a_log), d_lanes, nw2)


def _flash_kernel(*refs, tile, shared_key, forget, heads_per_cum_block):
    refs = list(refs)
    q_ref, k_ref = refs[:2]
    pos = 2
    kr_ref = ck_ref = None
    if shared_key:
        kr_ref = refs[pos]
        pos += 1
    v_ref = refs[pos]
    pos += 1
    if forget:
        ck_ref = refs[pos]
        pos += 1
    z_ref, o_ref = refs[pos], refs[pos + 1]

    if forget:
        cum_row = pl.program_id(1) % heads_per_cum_block
    n_tiles = q_ref.shape[0] // tile
    dv = v_ref.shape[1]
    row = lax.broadcasted_iota(jnp.int32, (tile, tile), 0)
    col = lax.broadcasted_iota(jnp.int32, (tile, tile), 1)
    on_or_below_diagonal = row >= col

    def logits(q, k_rows):
        k = k_ref[k_rows, :]
        if shared_key:
            k = jnp.concatenate([k, kr_ref[k_rows, :]], axis=1)
        s = lax.dot_general(q, k, (((1,), (1,)), ((), ())), preferred_element_type=F32)
        if forget:
            s = s - ck_ref[pl.ds(cum_row, 1), k_rows]
        return s

    order = list(reversed(range(n_tiles)))
    scores = {}
    for qi in order:
        q_rows = slice(qi * tile, (qi + 1) * tile)
        q = q_ref[q_rows, :]
        scores[qi] = (jnp.where(on_or_below_diagonal, logits(q, q_rows), NEG_BIG),
                      logits(q, slice(0, qi * tile)) if qi > 0 else None)
    for qi in order:
        q_rows = slice(qi * tile, (qi + 1) * tile)
        before = slice(0, qi * tile)
        s_diag, s_before = scores[qi]
        m = jnp.max(s_diag, axis=1, keepdims=True)
        if qi > 0:
            m = jnp.maximum(m, jnp.max(s_before, axis=1, keepdims=True))
        p_diag = jnp.exp2(s_diag - m)
        l = jnp.sum(p_diag, axis=1, keepdims=True)
        acc = jnp.dot(p_diag.astype(BF16), v_ref[q_rows, :], preferred_element_type=F32)
        if qi > 0:
            p_before = jnp.exp2(s_before - m)
            l = l + jnp.sum(p_before, axis=1, keepdims=True)
            acc = acc + jnp.dot(p_before.astype(BF16), v_ref[before, :], preferred_element_type=F32)
        o = acc * (1.0 / l)
        o_ref[q_rows, :] = (o * _silu(z_ref[q_rows, :].astype(F32))).astype(o_ref.dtype)


def _flash(q_arr, q_width, q_blk0, k_arr, k_blk0, v_arr, v_blk0, z_arr, z_blk0, batch, seq, heads,
           tile, name, kr_arr=None, cum_t=None):
    m = q_arr.shape[0]
    dv = LANES
    in_specs = [pl.BlockSpec((seq, q_width), lambda b, h: (b, q_blk0 + h)),
                pl.BlockSpec((seq, LANES), lambda b, h: (b, k_blk0 + h))]
    args = [q_arr, k_arr]
    if kr_arr is not None:
        in_specs.append(pl.BlockSpec((seq, LANES), lambda b, h: (b, 0)))
        args.append(kr_arr)
    in_specs.append(pl.BlockSpec((seq, dv), lambda b, h: (b, v_blk0 + h)))
    args.append(v_arr)
    cum_rows = 8
    if cum_t is not None:
        in_specs.append(pl.BlockSpec((None, cum_rows, seq), lambda b, h: (b, h // cum_rows, 0)))
        args.append(cum_t)
    in_specs.append(pl.BlockSpec((seq, dv), lambda b, h: (b, z_blk0 + h)))
    args.append(z_arr)
    return pl.pallas_call(
        functools.partial(_flash_kernel, tile=min(tile, seq), shared_key=kr_arr is not None,
                          forget=cum_t is not None, heads_per_cum_block=cum_rows),
        grid=(batch, heads),
        in_specs=in_specs,
        out_specs=pl.BlockSpec((seq, dv), lambda b, h: (b, h)),
        out_shape=jax.ShapeDtypeStruct((m, heads * dv), BF16),
        compiler_params=_params(2),
        name=name,
    )(*args)


def _forget_cum_kernel(f_ref, b_ref, o_ref):
    x = f_ref[...] + b_ref[...]
    log_f = jnp.minimum(x, 0.0) - jnp.log1p(jnp.exp(-jnp.abs(x)))
    o_ref[...] = (_cumsum_rows(log_f) * LOG2E).T


def _forget_cum(f_raw, f_bias_pad, batch, seq):
    return pl.pallas_call(
        _forget_cum_kernel,
        grid=(batch,),
        in_specs=[pl.BlockSpec((seq, LANES), lambda b: (b, 0)), pl.BlockSpec((1, LANES), lambda b: (0, 0))],
        out_specs=pl.BlockSpec((None, LANES, seq), lambda b: (b, 0, 0)),
        out_shape=jax.ShapeDtypeStruct((batch, LANES, seq), F32),
        compiler_params=_params(1),
        name="forget_cum",
    )(f_raw, f_bias_pad)


def _dilated_kernel(*refs, seq, pre_regrouped):
    n_grp = len(DIL_CONFIGS)
    qkv_refs = refs[:3 * n_grp]
    z_ref, o_ref = refs[3 * n_grp], refs[3 * n_grp + 1]
    scratch = list(refs[3 * n_grp + 2:])
    take = lambda count: [scratch.pop(0) for _ in range(count)]
    out_s = take(n_grp)
    lse_s = take(n_grp)
    stage_s = take(3)
    strided = [g for g, (_, dil) in enumerate(DIL_CONFIGS) if dil > 1]
    regroup_here = [g for g in strided if g not in pre_regrouped]
    qkv_res = {g: take(3) for g in regroup_here}
    for g in pre_regrouped:
        qkv_res[g] = qkv_refs[3 * g:3 * g + 3]
    out_res = {g: take(1)[0] for g in strided}
    lse_res = {g: take(1)[0] for g in strided}

    n_copy = 0
    for g in regroup_here:
        dil = DIL_CONFIGS[g][1]
        length = seq // dil
        for which in range(3):
            stage = stage_s[n_copy % len(stage_s)]
            n_copy += 1
            stage[...] = qkv_refs[3 * g + which][...].astype(F32)
            for residue in range(dil):
                qkv_res[g][which][residue * length:(residue + 1) * length, :] = (
                    stage[pl.ds(residue, length, stride=dil), :].astype(BF16))

    for g, (window, dil) in enumerate(DIL_CONFIGS):
        span = window // dil
        length = seq // dil
        blk = min(span, length)
        n_blk = length // blk
        q_src, k_src, v_src = qkv_res[g] if dil > 1 else qkv_refs[3 * g:3 * g + 3]
        o_dst, l_dst = (out_res[g], lse_res[g]) if dil > 1 else (out_s[g], lse_s[g])
        row1 = lax.broadcasted_iota(jnp.int32, (blk, blk), 0)
        col1 = lax.broadcasted_iota(jnp.int32, (blk, blk), 1)
        band_first = (row1 >= col1) & (row1 - col1 <= span)
        row2 = lax.broadcasted_iota(jnp.int32, (blk, 2 * blk), 0)
        col2 = lax.broadcasted_iota(jnp.int32, (blk, 2 * blk), 1)
        dist = row2 + blk - col2
        band = (dist >= 0) & (dist <= span)
        if n_blk == 1:
            as_blocks = lambda ref: ref[...].reshape(dil, blk, LANES)
            s = jnp.einsum("rqd,rkd->rqk", as_blocks(q_src), as_blocks(k_src), preferred_element_type=F32)
            s = jnp.where(band_first[None], s, NEG_BIG)
            m = jnp.max(s, axis=2, keepdims=True)
            p = jnp.exp2(s - m)
            den = jnp.sum(p, axis=2, keepdims=True)
            o = jnp.einsum("rqk,rkd->rqd", p.astype(BF16), as_blocks(v_src), preferred_element_type=F32)
            o_dst[...] = (o * (1.0 / den)).reshape(seq, LANES)
            l_dst[...] = jnp.broadcast_to(m + jnp.log2(den), (dil, blk, LANES)).reshape(seq, LANES)
            continue
        for residue in range(dil):
            for n in range(n_blk):
                rows = slice(residue * length + n * blk, residue * length + (n + 1) * blk)
                keys = rows if n == 0 else slice(rows.start - blk, rows.stop)
                s = lax.dot_general(q_src[rows, :], k_src[keys, :], (((1,), (1,)), ((), ())),
                                    preferred_element_type=F32)
                s = jnp.where(band_first if n == 0 else band, s, NEG_BIG)
                m = jnp.max(s, axis=1, keepdims=True)
                p = jnp.exp2(s - m)
                den = jnp.sum(p, axis=1, keepdims=True)
                o_dst[rows, :] = jnp.dot(p.astype(BF16), v_src[keys, :], preferred_element_type=F32) * (1.0 / den)
                l_dst[rows, :] = jnp.broadcast_to(m + jnp.log2(den), (blk, LANES))

    for g in strided:
        dil = DIL_CONFIGS[g][1]
        length = seq // dil
        for residue in range(dil):
            src_rows = slice(residue * length, (residue + 1) * length)
            out_s[g][pl.ds(residue, length, stride=dil), :] = out_res[g][src_rows, :]
            lse_s[g][pl.ds(residue, length, stride=dil), :] = lse_res[g][src_rows, :]

    rows_per_step = min(256, seq)
    for i in range(seq // rows_per_step):
        rs = slice(i * rows_per_step, (i + 1) * rows_per_step)
        lses = [lse_s[g][rs, :] for g in range(n_grp)]
        lse_max = functools.reduce(jnp.maximum, lses)
        weights = [jnp.exp2(l - lse_max) for l in lses]
        num = sum(w * out_s[g][rs, :] for g, w in enumerate(weights))
        o = num * (1.0 / sum(weights))
        o_ref[rs, :] = (o * _silu(z_ref[rs, :].astype(F32))).astype(o_ref.dtype)


def _dilated_mixer(qkv_operands, z_operand, pre_regrouped, batch, seq):
    m = z_operand[0].shape[0]
    n_grp = len(DIL_CONFIGS)
    n_strided = sum(dil > 1 for _, dil in DIL_CONFIGS)
    n_regroup = n_strided - len(pre_regrouped)
    head_block = lambda blk0: pl.BlockSpec((seq, LANES), lambda b, h: (b, blk0 + h))
    operands = list(qkv_operands) + [z_operand]
    in_specs = [head_block(blk0) for _, blk0 in operands]
    args = [arr for arr, _ in operands]
    return pl.pallas_call(
        functools.partial(_dilated_kernel, seq=seq, pre_regrouped=tuple(pre_regrouped)),
        grid=(batch, DIL_N_HEADS),
        in_specs=in_specs,
        out_specs=pl.BlockSpec((seq, LANES), lambda b, h: (b, h)),
        out_shape=jax.ShapeDtypeStruct((m, DIL_WIDTH), BF16),
        scratch_shapes=([pltpu.VMEM((seq, LANES), F32)] * (2 * n_grp + 3)
                        + [pltpu.VMEM((seq, LANES), BF16)] * 3 * n_regroup
                        + [pltpu.VMEM((seq, LANES), F32)] * 2 * n_strided),
        compiler_params=_params(2),
        name="dilated_mixer",
    )(*args)


def _ssd_layer(u, h, next_norm_w, final, in_w, conv_w, conv_b, dt_bias, a_log, d_skip, norm_w, out_w,
               batch, seq):
    w_main = in_w[:, :SSM_MAIN_COLS].astype(BF16)
    w_dt = jnp.pad(in_w[:, SSM_MAIN_COLS:], ((0, 0), (0, LANES - SSM_N_HEADS))).astype(BF16)
    proj = _mm(u, w_main, BF16, PROJ_ROW_TILE, PROJ_COL_TILE, "ssd_in_proj")
    dt_raw = _mm(u, w_dt, F32, PROJ_ROW_TILE, PROJ_COL_TILE, "ssd_dt_proj")
    y = _ssd_mixer(proj, dt_raw, conv_w, conv_b, dt_bias, a_log, d_skip, norm_w, batch, seq)
    return _out_proj(y, out_w.astype(BF16), h, next_norm_w, final)


def _mla_layer(u, h, next_norm_w, final, in_w, q_norm_w, kv_norm_w, uq_w, ukv_w, out_w, batch, seq):
    o1 = MLA_Q_LORA
    o2 = o1 + MLA_KV_LORA
    o3 = o2 + MLA_ROPE_DIM
    rope_perm = _rope_split_perm(MLA_ROPE_DIM)
    w_kr = jnp.pad(in_w[:, o2:o3], ((0, 0), (0, LANES - MLA_ROPE_DIM)))[:, rope_perm]
    w_small = jnp.concatenate([in_w[:, :o1], w_kr, in_w[:, o1:o2]], axis=1).astype(BF16)
    w_z = in_w[:, o3:].astype(BF16)
    q_scale = MLA_QK_DIM ** -0.5 * LOG2E
    w_q = (uq_w * q_scale).reshape(MLA_Q_LORA, MLA_N_HEADS, MLA_QK_DIM)
    w_q_rope = jnp.pad(w_q[:, :, MLA_NOPE_DIM:], ((0, 0), (0, 0), (0, LANES - MLA_ROPE_DIM)))[:, :, rope_perm]
    w_q = jnp.concatenate([w_q[:, :, :MLA_NOPE_DIM], w_q_rope], axis=2).reshape(MLA_Q_LORA, -1).astype(BF16)
    w_kv = ukv_w.reshape(MLA_KV_LORA, MLA_N_HEADS, 2, MLA_NOPE_DIM).transpose(0, 2, 1, 3)
    w_kv = w_kv.reshape(MLA_KV_LORA, -1).astype(BF16)
    tables = _rope_tables(seq, MLA_ROPE_DIM)

    small, k_rope = _mla_small(u, w_small, tables, seq)
    z = _mm(u, w_z, BF16, PROJ_ROW_TILE, PROJ_COL_TILE, "mla_gate_proj")
    q = _norm_mm(small, 0, MLA_Q_LORA, q_norm_w, w_q, BF16, PROJ_ROW_TILE // 2, 2 * PROJ_COL_TILE, "mla_q_proj",
                 rope_tables=tables, seq=seq)
    kv = _norm_mm(small, (MLA_SMALL_COLS - MLA_KV_LORA) // MLA_KV_LORA, MLA_KV_LORA, kv_norm_w, w_kv, BF16,
                  PROJ_ROW_TILE // 2, 2 * PROJ_COL_TILE, "mla_kv_proj")
    o = _flash(q, MLA_Q_PAD, 0, kv, 0, kv, MLA_N_HEADS, z, 0, batch, seq, MLA_N_HEADS, 512, "mla_attention",
               kr_arr=k_rope)
    return _out_proj(o, out_w.astype(BF16), h, next_norm_w, final)


def _fox_layer(u, h, next_norm_w, final, in_w, f_bias, out_w, batch, seq):
    w = FOX_WIDTH
    q_scale = FOX_HEAD_DIM ** -0.5 * LOG2E
    w_main = jnp.concatenate([in_w[:, :w] * q_scale, in_w[:, w:3 * w], in_w[:, 3 * w + FOX_N_HEADS:]],
                             axis=1).astype(BF16)
    w_f = jnp.pad(in_w[:, 3 * w:3 * w + FOX_N_HEADS], ((0, 0), (0, LANES - FOX_N_HEADS))).astype(BF16)
    f_bias_pad = jnp.pad(f_bias.astype(F32), (0, LANES - FOX_N_HEADS)).reshape(1, LANES)
    proj = _mm(u, w_main, BF16, PROJ_ROW_TILE, PROJ_COL_TILE, "fox_in_proj")
    f_raw = _mm(u, w_f, F32, PROJ_ROW_TILE, LANES, "fox_forget_proj")
    cum_t = _forget_cum(f_raw, f_bias_pad, batch, seq)
    hh = FOX_N_HEADS
    o = _flash(proj, LANES, 0, proj, hh, proj, 2 * hh, proj, 3 * hh, batch, seq, hh, 512, "fox_attention",
               cum_t=cum_t)
    return _out_proj(o, out_w.astype(BF16), h, next_norm_w, final)


def _dilated_layer(u, h, next_norm_w, final, in_w, out_w, batch, seq):
    w = DIL_WIDTH
    q_scale = DIL_HEAD_DIM ** -0.5 * LOG2E
    rope_perm = _rope_split_perm(DIL_ROPE_DIM)

    def split_rotary(cols):
        return cols.reshape(D_MODEL, DIL_N_HEADS, DIL_HEAD_DIM)[:, :, rope_perm].reshape(D_MODEL, w)

    n_grp = len(DIL_CONFIGS)
    qk_cols, v_cols = [], []
    for gi in range(n_grp):
        base = 3 * w * gi
        qk_cols += [split_rotary(in_w[:, base:base + w] * q_scale), split_rotary(in_w[:, base + w:base + 2 * w])]
        v_cols.append(in_w[:, base + 2 * w:base + 3 * w])
    w_gate = in_w[:, 3 * w * n_grp:]
    tables = _rope_tables(seq, DIL_ROPE_DIM)

    last = n_grp - 1
    dil = DIL_CONFIGS[last][1]

    def residue_major(a, lead):
        cols = a.shape[-1]
        return a.reshape(lead, seq // dil, dil, cols).transpose(0, 2, 1, 3).reshape(lead * seq, cols)

    u_res = residue_major(u, batch)
    tables_res = tuple(residue_major(t, 1) for t in tables)
    w_qk = jnp.concatenate(qk_cols[:2 * last], axis=1).astype(BF16)
    w_qk_last = jnp.concatenate(qk_cols[2 * last:], axis=1).astype(BF16)
    w_vz = jnp.concatenate(v_cols[:last] + [w_gate], axis=1).astype(BF16)
    qk = _mm_rope(u, w_qk, tables, seq, BF16, PROJ_ROW_TILE, PROJ_COL_TILE, "dilated_qk_proj")
    qk_last = _mm_rope(u_res, w_qk_last, tables_res, seq, BF16, PROJ_ROW_TILE, PROJ_COL_TILE,
                       "dilated_qk_proj_regrouped")
    vz = _mm(u, w_vz, BF16, PROJ_ROW_TILE, PROJ_COL_TILE, "dilated_vz_proj")
    v_last = _mm(u_res, v_cols[last].astype(BF16), BF16, PROJ_ROW_TILE, PROJ_COL_TILE, "dilated_v_proj_regrouped")
    hb = DIL_N_HEADS
    operands = []
    for gi in range(last):
        operands += [(qk, 2 * hb * gi), (qk, 2 * hb * gi + hb), (vz, hb * gi)]
    operands += [(qk_last, 0), (qk_last, hb), (v_last, 0)]
    o = _dilated_mixer(operands, (vz, hb * last), (last,), batch, seq)
    return _out_proj(o, out_w.astype(BF16), h, next_norm_w, final)


def kernel(x, norm_w, final_norm_w, ssm_in_w, ssm_conv_w, ssm_conv_b, ssm_dt_bias, ssm_A_log, ssm_D,
           ssm_norm_w, ssm_out_w, mla_in_w, mla_q_norm_w, mla_kv_norm_w, mla_uq_w, mla_ukv_w, mla_out_w,
           fox_in_w, fox_f_bias, fox_out_w, dil_in_w, dil_out_w):
    batch, seq, d = x.shape
    depth = norm_w.shape[0]
    h = x.reshape(batch * seq, d)
    u = _rmsnorm(h, norm_w[0], BF16)
    for i in range(depth):
        kind, j = i % 4, i // 4
        final = i == depth - 1
        next_w = final_norm_w if final else norm_w[i + 1]
        if kind == 0:
            res = _ssd_layer(u, h, next_w, final, ssm_in_w[j], ssm_conv_w[j], ssm_conv_b[j], ssm_dt_bias[j],
                             ssm_A_log[j], ssm_D[j], ssm_norm_w[j], ssm_out_w[j], batch, seq)
        elif kind == 1:
            res = _mla_layer(u, h, next_w, final, mla_in_w[j], mla_q_norm_w[j], mla_kv_norm_w[j], mla_uq_w[j],
                             mla_ukv_w[j], mla_out_w[j], batch, seq)
        elif kind == 2:
            res = _fox_layer(u, h, next_w, final, fox_in_w[j], fox_f_bias[j], fox_out_w[j], batch, seq)
        else:
            res = _dilated_layer(u, h, next_w, final, dil_in_w[j], dil_out_w[j], batch, seq)
        if final:
            return res.reshape(batch, seq, d)
        h, u = res
```

```python
import functools
import math

import jax
import jax.numpy as jnp
from jax import lax
from jax.experimental import pallas as pl
from jax.experimental.pallas import tpu as pltpu

F32 = jnp.float32
BF16 = jnp.bfloat16

D_MODEL = 1024
RMS_EPS = 1e-6
ROPE_THETA = 500000.0
LOG2E = 1.4426950408889634
NEG_BIG = -1e30

LANES = 128

SSM_D_INNER = 2048
SSM_HEAD_DIM = 64
SSM_N_HEADS = 32
SSM_D_STATE = 128
SSM_N_GROUPS = 8
SSM_HEADS_PER_GROUP = SSM_N_HEADS // SSM_N_GROUPS
SSM_GROUP_WIDTH = SSM_HEADS_PER_GROUP * SSM_HEAD_DIM
SSM_CHUNK = 128
SSM_CONV = 4
SSM_BC_DIM = SSM_N_GROUPS * SSM_D_STATE
SSM_MAIN_COLS = 2 * SSM_D_INNER + 2 * SSM_BC_DIM
CONV_HALO = 16
SSD_CHUNKS_PER_STEP = 4

MLA_N_HEADS = 16
MLA_Q_LORA = 384
MLA_KV_LORA = 256
MLA_NOPE_DIM = 128
MLA_ROPE_DIM = 64
MLA_V_DIM = 128
MLA_QK_DIM = MLA_NOPE_DIM + MLA_ROPE_DIM
MLA_WIDTH = MLA_N_HEADS * MLA_V_DIM
MLA_Q_PAD = 2 * LANES
MLA_SMALL_COLS = MLA_Q_LORA + LANES + MLA_KV_LORA

FOX_N_HEADS = 16
FOX_HEAD_DIM = 128
FOX_WIDTH = FOX_N_HEADS * FOX_HEAD_DIM

DIL_CONFIGS = ((128, 1), (512, 4), (2048, 16))
DIL_N_HEADS = 8
DIL_HEAD_DIM = 128
DIL_WIDTH = DIL_N_HEADS * DIL_HEAD_DIM
DIL_ROPE_DIM = DIL_HEAD_DIM // 4

VMEM_LIMIT_BYTES = 56 * 1024 * 1024
PROJ_ROW_TILE = 2048
PROJ_COL_TILE = 1024
OUT_PROJ_ROW_TILE = 1024
ATTN_TILE = 512


def _params(n_axes):
    return pltpu.CompilerParams(dimension_semantics=("arbitrary",) * n_axes,
                                vmem_limit_bytes=VMEM_LIMIT_BYTES)


def _silu(x):
    return x * jax.nn.sigmoid(x)


def _rms_normalize(x, w):
    ms = jnp.mean(x * x, axis=-1, keepdims=True)
    return (x * lax.rsqrt(ms + RMS_EPS)) * w


ROPE_PARTNER_SHIFT = LANES // 2


def _rope_block(x, cos_t, sin_t):
    return x * cos_t + pltpu.roll(x, ROPE_PARTNER_SHIFT, axis=1) * sin_t


def _rope_split_perm(rope_dim, width=LANES):
    half = rope_dim // 2
    first = list(range(half))
    second = list(range(half, rope_dim))
    rest = list(range(rope_dim, width))
    n_fill = ROPE_PARTNER_SHIFT - half
    return jnp.array(first + rest[:n_fill] + second + rest[n_fill:], jnp.int32)


def _rope_tables(seq, rope_dim):
    half = rope_dim // 2
    inv_freq = ROPE_THETA ** (-jnp.arange(half, dtype=F32) / half)
    ang = jnp.arange(seq, dtype=F32)[:, None] * inv_freq[None, :]
    cos, sin = jnp.cos(ang), jnp.sin(ang)
    ones = jnp.ones((seq, ROPE_PARTNER_SHIFT - half), F32)
    zeros = jnp.zeros((seq, ROPE_PARTNER_SHIFT - half), F32)
    cos_t = jnp.concatenate([cos, ones, cos, ones], axis=1)
    sin_t = jnp.concatenate([-sin, zeros, sin, zeros], axis=1)
    return cos_t, sin_t


def _rmsnorm_kernel(x_ref, w_ref, o_ref):
    o_ref[...] = _rms_normalize(x_ref[...], w_ref[...]).astype(o_ref.dtype)


def _rmsnorm(x, w, out_dtype, tm=1024):
    m, d = x.shape
    tm = min(tm, m)
    return pl.pallas_call(
        _rmsnorm_kernel,
        grid=(m // tm,),
        in_specs=[pl.BlockSpec((tm, d), lambda i: (i, 0)), pl.BlockSpec((1, d), lambda i: (0, 0))],
        out_specs=pl.BlockSpec((tm, d), lambda i: (i, 0)),
        out_shape=jax.ShapeDtypeStruct((m, d), out_dtype),
        compiler_params=_params(1),
        name="rmsnorm",
    )(x, w.reshape(1, d))


def _mm_kernel(a_ref, w_ref, o_ref):
    o_ref[...] = jnp.dot(a_ref[...], w_ref[...], preferred_element_type=F32).astype(o_ref.dtype)


def _mm(a, w, out_dtype, tm, tn, name):
    m, k = a.shape
    n = w.shape[1]
    tm = min(tm, m)
    tn = min(tn, n)
    return pl.pallas_call(
        _mm_kernel,
        grid=(m // tm, n // tn),
        in_specs=[pl.BlockSpec((tm, k), lambda i, j: (i, 0)), pl.BlockSpec((k, tn), lambda i, j: (0, j))],
        out_specs=pl.BlockSpec((tm, tn), lambda i, j: (i, j)),
        out_shape=jax.ShapeDtypeStruct((m, n), out_dtype),
        compiler_params=_params(2),
        name=name,
    )(a, w)


def _mm_rope_kernel(a_ref, w_ref, cos_ref, sin_ref, o_ref):
    acc = jnp.dot(a_ref[...], w_ref[...], preferred_element_type=F32)
    cos_t, sin_t = cos_ref[...], sin_ref[...]
    for blk in range(acc.shape[1] // LANES):
        cols = slice(blk * LANES, (blk + 1) * LANES)
        o_ref[:, cols] = _rope_block(acc[:, cols], cos_t, sin_t).astype(o_ref.dtype)


def _mm_rope(a, w, tables, seq, out_dtype, tm, tn, name):
    m, k = a.shape
    n = w.shape[1]
    tm = min(tm, seq)
    row_blocks = seq // tm
    tab_spec = pl.BlockSpec((tm, LANES), lambda i, j: (i % row_blocks, 0))
    return pl.pallas_call(
        _mm_rope_kernel,
        grid=(m // tm, n // tn),
        in_specs=[pl.BlockSpec((tm, k), lambda i, j: (i, 0)), pl.BlockSpec((k, tn), lambda i, j: (0, j)),
                  tab_spec, tab_spec],
        out_specs=pl.BlockSpec((tm, tn), lambda i, j: (i, j)),
        out_shape=jax.ShapeDtypeStruct((m, n), out_dtype),
        compiler_params=_params(2),
        name=name,
    )(a, w, *tables)


def _mla_small_kernel(a_ref, w_ref, qn_ref, kvn_ref, cos_ref, sin_ref, cq_ref, ckv_ref, kr_ref):
    acc = jnp.dot(a_ref[...], w_ref[...], preferred_element_type=F32)
    kv0 = MLA_Q_LORA + LANES
    cq_ref[...] = _rms_normalize(acc[:, :MLA_Q_LORA], qn_ref[...]).astype(cq_ref.dtype)
    ckv_ref[...] = _rms_normalize(acc[:, kv0:kv0 + MLA_KV_LORA], kvn_ref[...]).astype(ckv_ref.dtype)
    kr_ref[...] = _rope_block(acc[:, MLA_Q_LORA:kv0], cos_ref[...], sin_ref[...]).astype(kr_ref.dtype)


def _mla_small(u, w, q_norm_w, kv_norm_w, tables, seq, tm=1024):
    m, k = u.shape
    tm = min(tm, seq)
    row_blocks = seq // tm
    tab_spec = pl.BlockSpec((tm, LANES), lambda i: (i % row_blocks, 0))
    rows = lambda width: pl.BlockSpec((tm, width), lambda i: (i, 0))
    whole = lambda r, c: pl.BlockSpec((r, c), lambda i: (0, 0))
    return pl.pallas_call(
        _mla_small_kernel,
        grid=(m // tm,),
        in_specs=[rows(k), whole(k, MLA_SMALL_COLS), whole(1, MLA_Q_LORA), whole(1, MLA_KV_LORA),
                  tab_spec, tab_spec],
        out_specs=[rows(MLA_Q_LORA), rows(MLA_KV_LORA), rows(LANES)],
        out_shape=[jax.ShapeDtypeStruct((m, MLA_Q_LORA), BF16), jax.ShapeDtypeStruct((m, MLA_KV_LORA), BF16),
                   jax.ShapeDtypeStruct((m, LANES), BF16)],
        compiler_params=_params(1),
        name="mla_in_small",
    )(u, w, q_norm_w.reshape(1, -1), kv_norm_w.reshape(1, -1), *tables)


def _out_proj_kernel(a_ref, w_ref, h_ref, nw_ref, *out_refs, final):
    h_new = h_ref[...] + jnp.dot(a_ref[...], w_ref[...], preferred_element_type=F32)
    if final:
        (u_ref,) = out_refs
    else:
        hn_ref, u_ref = out_refs
        hn_ref[...] = h_new
    u_ref[...] = _rms_normalize(h_new, nw_ref[...]).astype(u_ref.dtype)


def _out_proj(a, w, h, norm_w, final, tm=OUT_PROJ_ROW_TILE):
    m, k = a.shape
    d = w.shape[1]
    tm = min(tm, m)
    row_spec = pl.BlockSpec((tm, d), lambda i: (i, 0))
    if final:
        out_specs, out_shape = row_spec, jax.ShapeDtypeStruct((m, d), F32)
    else:
        out_specs = [row_spec, row_spec]
        out_shape = [jax.ShapeDtypeStruct((m, d), F32), jax.ShapeDtypeStruct((m, d), BF16)]
    return pl.pallas_call(
        functools.partial(_out_proj_kernel, final=final),
        grid=(m // tm,),
        in_specs=[pl.BlockSpec((tm, k), lambda i: (i, 0)), pl.BlockSpec((k, d), lambda i: (0, 0)),
                  row_spec, pl.BlockSpec((1, d), lambda i: (0, 0))],
        out_specs=out_specs,
        out_shape=out_shape,
        compiler_params=_params(1),
        name="out_proj_final" if final else "out_proj",
    )(a, w, h, norm_w.reshape(1, d))


def _expand_heads(v):
    t = v.shape[0]
    cols = [jnp.broadcast_to(v[:, h:h + 1], (t, LANES)) for h in range(SSM_HEADS_PER_GROUP)]
    lane = lax.broadcasted_iota(jnp.int32, (t, LANES), 1)
    first = lane < SSM_HEAD_DIM
    return jnp.concatenate([jnp.where(first, cols[0], cols[1]), jnp.where(first, cols[2], cols[3])], axis=1)


def _cumsum_rows(x):
    n = x.shape[0]
    row = lax.broadcasted_iota(jnp.int32, x.shape, 0)
    shift = 1
    while shift < n:
        x = x + jnp.where(row >= shift, pltpu.roll(x, shift, axis=0), 0.0)
        shift *= 2
    return x


def _ssd_kernel(z_ref, x_ref, b_ref, c_ref, dt_ref, wx_ref, wb_ref, wc_ref, bx_ref, bb_ref, bc_ref,
                dtb_ref, alog_ref, dskip_ref, nw_ref, o_ref, *, seq):
    t = SSM_CHUNK
    n_chunks = seq // t

    gw, ns = SSM_GROUP_WIDTH, SSM_D_STATE
    conv_w = jnp.concatenate([wx_ref[...], wb_ref[...], wc_ref[...]], axis=1)
    conv_b = jnp.concatenate([bx_ref[...], bb_ref[...], bc_ref[...]], axis=1)
    ext_rows = 2 * t
    n_shift = SSM_CONV - 1
    sel_row = lax.broadcasted_iota(jnp.int32, (n_shift * t, ext_rows), 0)
    sel_col = lax.broadcasted_iota(jnp.int32, (n_shift * t, ext_rows), 1)
    shift_select = (sel_col == CONV_HALO + sel_row % t - (n_shift - sel_row // t)).astype(BF16)
    ext_fill = jnp.zeros((ext_rows - t - CONV_HALO, gw + 2 * ns), BF16)

    def conv_shifts(t0, c):
        rows = pl.ds(t0, t)
        cur = jnp.concatenate([x_ref[rows, :], b_ref[rows, :], c_ref[rows, :]], axis=1)
        halo = pl.ds(pl.multiple_of(jnp.maximum(t0 - CONV_HALO, 0), CONV_HALO), CONV_HALO)
        prev = jnp.concatenate([x_ref[halo, :], b_ref[halo, :], c_ref[halo, :]], axis=1)
        prev = jnp.where(c > 0, prev, jnp.zeros_like(prev))
        ext = jnp.concatenate([prev, cur, ext_fill], axis=0)
        return cur, jnp.dot(shift_select, ext, preferred_element_type=F32)

    def conv_silu(cur, shifted):
        acc = conv_b + conv_w[n_shift:n_shift + 1, :] * cur.astype(F32)
        for tap in range(n_shift):
            acc = acc + conv_w[tap:tap + 1, :] * shifted[tap * t:(tap + 1) * t, :]
        act = _silu(acc)
        return act[:, :gw], act[:, gw:gw + ns], act[:, gw + ns:]

    row = lax.broadcasted_iota(jnp.int32, (t, t), 0)
    col = lax.broadcasted_iota(jnp.int32, (t, t), 1)
    causal = row >= col
    head_of_lane = lax.broadcasted_iota(jnp.int32, (t, SSM_GROUP_WIDTH), 1) // SSM_HEAD_DIM
    neg_a = -jnp.exp(alog_ref[...])
    group_lane_shift = (LANES - SSM_HEADS_PER_GROUP * pl.program_id(1)) % LANES

    per_step = SSD_CHUNKS_PER_STEP if n_chunks % SSD_CHUNKS_PER_STEP == 0 else 1

    def chunk_group(i, state):
        chunks = [i * per_step + j for j in range(per_step)]
        starts = [pl.multiple_of(c * t, t) for c in chunks]
        conv_in = [conv_shifts(t0, c) for t0, c in zip(starts, chunks)]

        stage2 = []
        for t0, (cur, shifted) in zip(starts, conv_in):
            x, b_in, c_out = conv_silu(cur, shifted)
            dt_raw = pltpu.roll(dt_ref[pl.ds(t0, t), :], group_lane_shift, axis=1)
            dt = jax.nn.softplus(dt_raw + dtb_ref[...])
            a_cum = _cumsum_rows(dt * neg_a)
            a_cum_wide = _expand_heads(a_cum)
            exp_cum = jnp.exp(a_cum_wide)
            decay_to_end = jnp.exp(a_cum_wide[t - 1:t, :] - a_cum_wide)
            xdt = x * _expand_heads(dt)
            c_bf = c_out.astype(BF16)
            cb = lax.dot_general(c_bf, b_in.astype(BF16), (((1,), (1,)), ((), ())),
                                 preferred_element_type=F32)
            stage2.append((x, b_in, c_bf, a_cum, exp_cum, decay_to_end, xdt, cb))

        stage3 = []
        for x, b_in, c_bf, a_cum, exp_cum, decay_to_end, xdt, cb in stage2:
            a_cum_t = a_cum.T
            y_diag = None
            for h in range(SSM_HEADS_PER_GROUP):
                seg = jnp.broadcast_to(a_cum[:, h:h + 1], (t, t)) - jnp.broadcast_to(a_cum_t[h:h + 1, :], (t, t))
                decay = jnp.exp(jnp.where(causal, seg, NEG_BIG))
                x_h = jnp.where(head_of_lane == h, xdt, 0.0).astype(BF16)
                term = jnp.dot((cb * decay).astype(BF16), x_h, preferred_element_type=F32)
                y_diag = term if y_diag is None else y_diag + term
            stage3.append((x, c_bf, exp_cum, y_diag, b_in.T.astype(BF16), (xdt * decay_to_end).astype(BF16)))

        for t0, (x, c_bf, exp_cum, y_diag, b_t, x_to_end) in zip(starts, stage3):
            y = y_diag + jnp.dot(c_bf, state.astype(BF16), preferred_element_type=F32) * exp_cum
            state = state * exp_cum[t - 1:t, :] + jnp.dot(b_t, x_to_end, preferred_element_type=F32)
            y = y + dskip_ref[...] * x
            g = y * _silu(z_ref[pl.ds(t0, t), :].astype(F32))
            o_ref[pl.ds(t0, t), :] = _rms_normalize(g, nw_ref[...]).astype(o_ref.dtype)
        return state

    lax.fori_loop(0, n_chunks // per_step, chunk_group, jnp.zeros((SSM_D_STATE, SSM_GROUP_WIDTH), F32))


def _ssd_mixer(proj, dt_raw, conv_w, conv_b, dt_bias, a_log, d_skip, norm_w, batch, seq):
    m = proj.shape[0]
    gw, ns = SSM_GROUP_WIDTH, SSM_D_STATE
    x_blk0 = SSM_D_INNER // gw
    b_blk0 = 2 * SSM_D_INNER // ns
    c_blk0 = b_blk0 + SSM_N_GROUPS
    cw_b0 = SSM_D_INNER // ns
    cw_c0 = cw_b0 + SSM_N_GROUPS

    def pad_heads(v):
        out = jnp.zeros((SSM_N_GROUPS, 1, LANES), F32)
        return out.at[:, 0, :SSM_HEADS_PER_GROUP].set(v.astype(F32).reshape(SSM_N_GROUPS, SSM_HEADS_PER_GROUP))

    d_lanes = jnp.repeat(d_skip.astype(F32).reshape(SSM_N_GROUPS, SSM_HEADS_PER_GROUP), SSM_HEAD_DIM,
                         axis=1).reshape(SSM_N_GROUPS, 1, gw)
    conv_b2 = conv_b.reshape(1, -1)
    nw2 = norm_w.reshape(1, -1)
    seq_spec = lambda width, blk0: pl.BlockSpec((seq, width), lambda b, g: (b, blk0 + g))
    row_spec = lambda rows, width, blk0: pl.BlockSpec((rows, width), lambda b, g: (0, blk0 + g))
    grp_spec = lambda width: pl.BlockSpec((None, 1, width), lambda b, g: (g, 0, 0))
    return pl.pallas_call(
        functools.partial(_ssd_kernel, seq=seq),
        grid=(batch, SSM_N_GROUPS),
        in_specs=[seq_spec(gw, 0), seq_spec(gw, x_blk0), seq_spec(ns, b_blk0), seq_spec(ns, c_blk0),
                  pl.BlockSpec((seq, LANES), lambda b, g: (b, 0)),
                  row_spec(SSM_CONV, gw, 0), row_spec(SSM_CONV, ns, cw_b0), row_spec(SSM_CONV, ns, cw_c0),
                  row_spec(1, gw, 0), row_spec(1, ns, cw_b0), row_spec(1, ns, cw_c0),
                  grp_spec(LANES), grp_spec(LANES), grp_spec(gw), row_spec(1, gw, 0)],
        out_specs=pl.BlockSpec((seq, gw), lambda b, g: (b, g)),
        out_shape=jax.ShapeDtypeStruct((m, SSM_D_INNER), BF16),
        compiler_params=_params(2),
        name="ssd_mixer",
    )(proj, proj, proj, proj, dt_raw, conv_w, conv_w, conv_w, conv_b2, conv_b2, conv_b2,
      pad_heads(dt_bias), pad_heads(a_log), d_lanes, nw2)


def _causal_attention(load_q, k_ref, v_ref, z_ref, o_ref, tile, key_bias=None):
    n_tiles = k_ref.shape[0] // tile
    row = lax.broadcasted_iota(jnp.int32, (tile, tile), 0)
    col = lax.broadcasted_iota(jnp.int32, (tile, tile), 1)
    on_or_below_diagonal = row >= col

    def logits(q, k_rows):
        s = lax.dot_general(q, k_ref[k_rows, :], (((1,), (1,)), ((), ())), preferred_element_type=F32)
        if key_bias is not None:
            s = s - key_bias(k_rows)
        return s

    order = list(reversed(range(n_tiles)))
    scores = {}
    for qi in order:
        q_rows = slice(qi * tile, (qi + 1) * tile)
        q = load_q(q_rows)
        scores[qi] = (jnp.where(on_or_below_diagonal, logits(q, q_rows), NEG_BIG),
                      logits(q, slice(0, qi * tile)) if qi > 0 else None)
    for qi in order:
        q_rows = slice(qi * tile, (qi + 1) * tile)
        before = slice(0, qi * tile)
        s_diag, s_before = scores[qi]
        m = jnp.max(s_diag, axis=1, keepdims=True)
        if qi > 0:
            m = jnp.maximum(m, jnp.max(s_before, axis=1, keepdims=True))
        p_diag = jnp.exp2(s_diag - m)
        l = jnp.sum(p_diag, axis=1, keepdims=True)
        acc = jnp.dot(p_diag.astype(BF16), v_ref[q_rows, :], preferred_element_type=F32)
        if qi > 0:
            p_before = jnp.exp2(s_before - m)
            l = l + jnp.sum(p_before, axis=1, keepdims=True)
            acc = acc + jnp.dot(p_before.astype(BF16), v_ref[before, :], preferred_element_type=F32)
        o = acc * (1.0 / l)
        o_ref[q_rows, :] = (o * _silu(z_ref[q_rows, :].astype(F32))).astype(o_ref.dtype)


CUM_ROWS_PER_BLOCK = 8


def _fox_attention_kernel(q_ref, k_ref, v_ref, cum_ref, z_ref, o_ref, *, tile):
    cum_row = pl.program_id(1) % CUM_ROWS_PER_BLOCK
    key_bias = lambda k_rows: cum_ref[pl.ds(cum_row, 1), k_rows]
    _causal_attention(lambda rows: q_ref[rows, :], k_ref, v_ref, z_ref, o_ref, tile, key_bias)


def _fox_attention(proj, cum_t, batch, seq, tile=ATTN_TILE):
    m = proj.shape[0]
    hh = FOX_N_HEADS
    head_block = lambda blk0: pl.BlockSpec((seq, LANES), lambda b, h: (b, blk0 + h))
    return pl.pallas_call(
        functools.partial(_fox_attention_kernel, tile=min(tile, seq)),
        grid=(batch, hh),
        in_specs=[head_block(0), head_block(hh), head_block(2 * hh),
                  pl.BlockSpec((None, CUM_ROWS_PER_BLOCK, seq), lambda b, h: (b, h // CUM_ROWS_PER_BLOCK, 0)),
                  head_block(3 * hh)],
        out_specs=pl.BlockSpec((seq, LANES), lambda b, h: (b, h)),
        out_shape=jax.ShapeDtypeStruct((m, FOX_WIDTH), BF16),
        compiler_params=_params(2),
        name="fox_attention",
    )(proj, proj, proj, cum_t, proj)


def _mla_attention_kernel(cq_ref, ckv_ref, wq_ref, wkv_ref, kr_ref, cos_ref, sin_ref, z_ref, o_ref,
                          q_s, k_s, v_s, *, tile):
    k_s[:, MLA_NOPE_DIM:] = kr_ref[...]
    for part in range(q_s.shape[0] // tile):
        rows = slice(part * tile, (part + 1) * tile)
        q = jnp.dot(cq_ref[rows, :], wq_ref[...], preferred_element_type=F32)
        q_s[rows, :MLA_NOPE_DIM] = q[:, :MLA_NOPE_DIM].astype(BF16)
        q_s[rows, MLA_NOPE_DIM:] = _rope_block(q[:, MLA_NOPE_DIM:], cos_ref[rows, :], sin_ref[rows, :]).astype(BF16)
        kv = jnp.dot(ckv_ref[rows, :], wkv_ref[...], preferred_element_type=F32)
        k_s[rows, :MLA_NOPE_DIM] = kv[:, :MLA_NOPE_DIM].astype(BF16)
        v_s[rows, :] = kv[:, MLA_NOPE_DIM:].astype(BF16)
    _causal_attention(lambda rows: q_s[rows, :], k_s, v_s, z_ref, o_ref, tile)


def _mla_attention(cq, ckv, w_q, w_kv, k_rope, tables, z, batch, seq, tile=ATTN_TILE):
    m = cq.shape[0]
    per_batch = lambda width: pl.BlockSpec((seq, width), lambda b, h: (b, 0))
    per_head = lambda rows: pl.BlockSpec((rows, MLA_Q_PAD), lambda b, h: (0, h))
    table = pl.BlockSpec((seq, LANES), lambda b, h: (0, 0))
    return pl.pallas_call(
        functools.partial(_mla_attention_kernel, tile=min(tile, seq)),
        grid=(batch, MLA_N_HEADS),
        in_specs=[per_batch(MLA_Q_LORA), per_batch(MLA_KV_LORA), per_head(MLA_Q_LORA), per_head(MLA_KV_LORA),
                  per_batch(LANES), table, table, pl.BlockSpec((seq, LANES), lambda b, h: (b, h))],
        out_specs=pl.BlockSpec((seq, LANES), lambda b, h: (b, h)),
        out_shape=jax.ShapeDtypeStruct((m, MLA_WIDTH), BF16),
        scratch_shapes=[pltpu.VMEM((seq, MLA_Q_PAD), BF16), pltpu.VMEM((seq, MLA_Q_PAD), BF16),
                        pltpu.VMEM((seq, MLA_V_DIM), BF16)],
        compiler_params=_params(2),
        name="mla_attention",
    )(cq, ckv, w_q, w_kv, k_rope, *tables, z)


def _forget_cum_kernel(f_ref, b_ref, o_ref):
    x = f_ref[...] + b_ref[...]
    log_f = jnp.minimum(x, 0.0) - jnp.log1p(jnp.exp(-jnp.abs(x)))
    o_ref[...] = (_cumsum_rows(log_f) * LOG2E).T


def _forget_cum(f_raw, f_bias_pad, batch, seq):
    return pl.pallas_call(
        _forget_cum_kernel,
        grid=(batch,),
        in_specs=[pl.BlockSpec((seq, LANES), lambda b: (b, 0)), pl.BlockSpec((1, LANES), lambda b: (0, 0))],
        out_specs=pl.BlockSpec((None, LANES, seq), lambda b: (b, 0, 0)),
        out_shape=jax.ShapeDtypeStruct((batch, LANES, seq), F32),
        compiler_params=_params(1),
        name="forget_cum",
    )(f_raw, f_bias_pad)


def _dilated_kernel(*refs, seq, pre_regrouped):
    n_grp = len(DIL_CONFIGS)
    qkv_refs = refs[:3 * n_grp]
    z_ref, o_ref = refs[3 * n_grp], refs[3 * n_grp + 1]
    scratch = list(refs[3 * n_grp + 2:])
    take = lambda count: [scratch.pop(0) for _ in range(count)]
    out_s = take(n_grp)
    lse_s = take(n_grp)
    stage_s = take(3)
    strided = [g for g, (_, dil) in enumerate(DIL_CONFIGS) if dil > 1]
    regroup_here = [g for g in strided if g not in pre_regrouped]
    qkv_res = {g: take(3) for g in regroup_here}
    for g in pre_regrouped:
        qkv_res[g] = qkv_refs[3 * g:3 * g + 3]
    out_res = {g: take(1)[0] for g in strided}
    lse_res = {g: take(1)[0] for g in strided}

    n_copy = 0
    for g in regroup_here:
        dil = DIL_CONFIGS[g][1]
        length = seq // dil
        for which in range(3):
            stage = stage_s[n_copy % len(stage_s)]
            n_copy += 1
            stage[...] = qkv_refs[3 * g + which][...].astype(F32)
            for residue in range(dil):
                qkv_res[g][which][residue * length:(residue + 1) * length, :] = (
                    stage[pl.ds(residue, length, stride=dil), :].astype(BF16))

    for g, (window, dil) in enumerate(DIL_CONFIGS):
        span = window // dil
        length = seq // dil
        blk = min(span, length)
        n_blk = length // blk
        q_src, k_src, v_src = qkv_res[g] if dil > 1 else qkv_refs[3 * g:3 * g + 3]
        o_dst, l_dst = (out_res[g], lse_res[g]) if dil > 1 else (out_s[g], lse_s[g])
        row1 = lax.broadcasted_iota(jnp.int32, (blk, blk), 0)
        col1 = lax.broadcasted_iota(jnp.int32, (blk, blk), 1)
        band_first = (row1 >= col1) & (row1 - col1 <= span)
        row2 = lax.broadcasted_iota(jnp.int32, (blk, 2 * blk), 0)
        col2 = lax.broadcasted_iota(jnp.int32, (blk, 2 * blk), 1)
        dist = row2 + blk - col2
        band = (dist >= 0) & (dist <= span)
        if n_blk == 1:
            as_blocks = lambda ref: ref[...].reshape(dil, blk, LANES)
            s = jnp.einsum("rqd,rkd->rqk", as_blocks(q_src), as_blocks(k_src), preferred_element_type=F32)
            s = jnp.where(band_first[None], s, NEG_BIG)
            m = jnp.max(s, axis=2, keepdims=True)
            p = jnp.exp2(s - m)
            den = jnp.sum(p, axis=2, keepdims=True)
            o = jnp.einsum("rqk,rkd->rqd", p.astype(BF16), as_blocks(v_src), preferred_element_type=F32)
            o_dst[...] = (o * (1.0 / den)).reshape(seq, LANES)
            l_dst[...] = jnp.broadcast_to(m + jnp.log2(den), (dil, blk, LANES)).reshape(seq, LANES)
            continue
        for residue in range(dil):
            for n in range(n_blk):
                rows = slice(residue * length + n * blk, residue * length + (n + 1) * blk)
                keys = rows if n == 0 else slice(rows.start - blk, rows.stop)
                s = lax.dot_general(q_src[rows, :], k_src[keys, :], (((1,), (1,)), ((), ())),
                                    preferred_element_type=F32)
                s = jnp.where(band_first if n == 0 else band, s, NEG_BIG)
                m = jnp.max(s, axis=1, keepdims=True)
                p = jnp.exp2(s - m)
                den = jnp.sum(p, axis=1, keepdims=True)
                o_dst[rows, :] = jnp.dot(p.astype(BF16), v_src[keys, :], preferred_element_type=F32) * (1.0 / den)
                l_dst[rows, :] = jnp.broadcast_to(m + jnp.log2(den), (blk, LANES))

    for g in strided:
        dil = DIL_CONFIGS[g][1]
        length = seq // dil
        for residue in range(dil):
            src_rows = slice(residue * length, (residue + 1) * length)
            out_s[g][pl.ds(residue, length, stride=dil), :] = out_res[g][src_rows, :]
            lse_s[g][pl.ds(residue, length, stride=dil), :] = lse_res[g][src_rows, :]

    rows_per_step = min(256, seq)
    for i in range(seq // rows_per_step):
        rs = slice(i * rows_per_step, (i + 1) * rows_per_step)
        lses = [lse_s[g][rs, :] for g in range(n_grp)]
        lse_max = functools.reduce(jnp.maximum, lses)
        weights = [jnp.exp2(l - lse_max) for l in lses]
        num = sum(w * out_s[g][rs, :] for g, w in enumerate(weights))
        o = num * (1.0 / sum(weights))
        o_ref[rs, :] = (o * _silu(z_ref[rs, :].astype(F32))).astype(o_ref.dtype)


def _dilated_mixer(qkv_operands, z_operand, pre_regrouped, batch, seq):
    m = z_operand[0].shape[0]
    n_grp = len(DIL_CONFIGS)
    n_strided = sum(dil > 1 for _, dil in DIL_CONFIGS)
    n_regroup = n_strided - len(pre_regrouped)
    head_block = lambda blk0: pl.BlockSpec((seq, LANES), lambda b, h: (b, blk0 + h))
    operands = list(qkv_operands) + [z_operand]
    in_specs = [head_block(blk0) for _, blk0 in operands]
    args = [arr for arr, _ in operands]
    return pl.pallas_call(
        functools.partial(_dilated_kernel, seq=seq, pre_regrouped=tuple(pre_regrouped)),
        grid=(batch, DIL_N_HEADS),
        in_specs=in_specs,
        out_specs=pl.BlockSpec((seq, LANES), lambda b, h: (b, h)),
        out_shape=jax.ShapeDtypeStruct((m, DIL_WIDTH), BF16),
        scratch_shapes=([pltpu.VMEM((seq, LANES), F32)] * (2 * n_grp + 3)
                        + [pltpu.VMEM((seq, LANES), BF16)] * 3 * n_regroup
                        + [pltpu.VMEM((seq, LANES), F32)] * 2 * n_strided),
        compiler_params=_params(2),
        name="dilated_mixer",
    )(*args)


def _ssd_layer(u, h, next_norm_w, final, in_w, conv_w, conv_b, dt_bias, a_log, d_skip, norm_w, out_w,
               batch, seq):
    w_main = in_w[:, :SSM_MAIN_COLS].astype(BF16)
    w_dt = jnp.pad(in_w[:, SSM_MAIN_COLS:], ((0, 0), (0, LANES - SSM_N_HEADS))).astype(BF16)
    proj = _mm(u, w_main, BF16, PROJ_ROW_TILE, PROJ_COL_TILE, "ssd_in_proj")
    dt_raw = _mm(u, w_dt, F32, PROJ_ROW_TILE, PROJ_COL_TILE, "ssd_dt_proj")
    y = _ssd_mixer(proj, dt_raw, conv_w, conv_b, dt_bias, a_log, d_skip, norm_w, batch, seq)
    return _out_proj(y, out_w.astype(BF16), h, next_norm_w, final)


def _mla_layer(u, h, next_norm_w, final, in_w, q_norm_w, kv_norm_w, uq_w, ukv_w, out_w, batch, seq):
    o1 = MLA_Q_LORA
    o2 = o1 + MLA_KV_LORA
    o3 = o2 + MLA_ROPE_DIM
    rope_perm = _rope_split_perm(MLA_ROPE_DIM)
    w_kr = jnp.pad(in_w[:, o2:o3], ((0, 0), (0, LANES - MLA_ROPE_DIM)))[:, rope_perm]
    w_small = jnp.concatenate([in_w[:, :o1], w_kr, in_w[:, o1:o2]], axis=1).astype(BF16)
    w_z = in_w[:, o3:].astype(BF16)
    q_scale = MLA_QK_DIM ** -0.5 * LOG2E
    w_q = (uq_w * q_scale).reshape(MLA_Q_LORA, MLA_N_HEADS, MLA_QK_DIM)
    w_q_rope = jnp.pad(w_q[:, :, MLA_NOPE_DIM:], ((0, 0), (0, 0), (0, LANES - MLA_ROPE_DIM)))[:, :, rope_perm]
    w_q = jnp.concatenate([w_q[:, :, :MLA_NOPE_DIM], w_q_rope], axis=2).reshape(MLA_Q_LORA, -1).astype(BF16)
    w_kv = ukv_w.astype(BF16)
    tables = _rope_tables(seq, MLA_ROPE_DIM)

    cq, ckv, k_rope = _mla_small(u, w_small, q_norm_w, kv_norm_w, tables, seq)
    z = _mm(u, w_z, BF16, PROJ_ROW_TILE, PROJ_COL_TILE, "mla_gate_proj")
    o = _mla_attention(cq, ckv, w_q, w_kv, k_rope, tables, z, batch, seq)
    return _out_proj(o, out_w.astype(BF16), h, next_norm_w, final)


def _fox_layer(u, h, next_norm_w, final, in_w, f_bias, out_w, batch, seq):
    w = FOX_WIDTH
    q_scale = FOX_HEAD_DIM ** -0.5 * LOG2E
    w_main = jnp.concatenate([in_w[:, :w] * q_scale, in_w[:, w:3 * w], in_w[:, 3 * w + FOX_N_HEADS:]],
                             axis=1).astype(BF16)
    w_f = jnp.pad(in_w[:, 3 * w:3 * w + FOX_N_HEADS], ((0, 0), (0, LANES - FOX_N_HEADS))).astype(BF16)
    f_bias_pad = jnp.pad(f_bias.astype(F32), (0, LANES - FOX_N_HEADS)).reshape(1, LANES)
    proj = _mm(u, w_main, BF16, PROJ_ROW_TILE, PROJ_COL_TILE, "fox_in_proj")
    f_raw = _mm(u, w_f, F32, PROJ_ROW_TILE, LANES, "fox_forget_proj")
    cum_t = _forget_cum(f_raw, f_bias_pad, batch, seq)
    o = _fox_attention(proj, cum_t, batch, seq)
    return _out_proj(o, out_w.astype(BF16), h, next_norm_w, final)


def _dilated_layer(u, h, next_norm_w, final, in_w, out_w, batch, seq):
    w = DIL_WIDTH
    q_scale = DIL_HEAD_DIM ** -0.5 * LOG2E
    rope_perm = _rope_split_perm(DIL_ROPE_DIM)

    def split_rotary(cols):
        return cols.reshape(D_MODEL, DIL_N_HEADS, DIL_HEAD_DIM)[:, :, rope_perm].reshape(D_MODEL, w)

    n_grp = len(DIL_CONFIGS)
    qk_cols, v_cols = [], []
    for gi in range(n_grp):
        base = 3 * w * gi
        qk_cols += [split_rotary(in_w[:, base:base + w] * q_scale), split_rotary(in_w[:, base + w:base + 2 * w])]
        v_cols.append(in_w[:, base + 2 * w:base + 3 * w])
    w_gate = in_w[:, 3 * w * n_grp:]
    tables = _rope_tables(seq, DIL_ROPE_DIM)

    last = n_grp - 1
    dil = DIL_CONFIGS[last][1]

    def residue_major(a, lead):
        cols = a.shape[-1]
        return a.reshape(lead, seq // dil, dil, cols).transpose(0, 2, 1, 3).reshape(lead * seq, cols)

    u_res = residue_major(u, batch)
    tables_res = tuple(residue_major(t, 1) for t in tables)
    w_qk = jnp.concatenate(qk_cols[:2 * last], axis=1).astype(BF16)
    w_qk_last = jnp.concatenate(qk_cols[2 * last:], axis=1).astype(BF16)
    w_vz = jnp.concatenate(v_cols[:last] + [w_gate], axis=1).astype(BF16)
    qk = _mm_rope(u, w_qk, tables, seq, BF16, PROJ_ROW_TILE, PROJ_COL_TILE, "dilated_qk_proj")
    qk_last = _mm_rope(u_res, w_qk_last, tables_res, seq, BF16, PROJ_ROW_TILE, PROJ_COL_TILE,
                       "dilated_qk_proj_regrouped")
    vz = _mm(u, w_vz, BF16, PROJ_ROW_TILE, PROJ_COL_TILE, "dilated_vz_proj")
    v_last = _mm(u_res, v_cols[last].astype(BF16), BF16, PROJ_ROW_TILE, PROJ_COL_TILE, "dilated_v_proj_regrouped")
    hb = DIL_N_HEADS
    operands = []
    for gi in range(last):
        operands += [(qk, 2 * hb * gi), (qk, 2 * hb * gi + hb), (vz, hb * gi)]
    operands += [(qk_last, 0), (qk_last, hb), (v_last, 0)]
    o = _dilated_mixer(operands, (vz, hb * last), (last,), batch, seq)
    return _out_proj(o, out_w.astype(BF16), h, next_norm_w, final)


def kernel(x, norm_w, final_norm_w, ssm_in_w, ssm_conv_w, ssm_conv_b, ssm_dt_bias, ssm_A_log, ssm_D,
           ssm_norm_w, ssm_out_w, mla_in_w, mla_q_norm_w, mla_kv_norm_w, mla_uq_w, mla_ukv_w, mla_out_w,
           fox_in_w, fox_f_bias, fox_out_w, dil_in_w, dil_out_w):
    batch, seq, d = x.shape
    depth = norm_w.shape[0]
    h = x.reshape(batch * seq, d)
    u = _rmsnorm(h, norm_w[0], BF16)
    for i in range(depth):
        kind, j = i % 4, i // 4
        final = i == depth - 1
        next_w = final_norm_w if final else norm_w[i + 1]
        if kind == 0:
            res = _ssd_layer(u, h, next_w, final, ssm_in_w[j], ssm_conv_w[j], ssm_conv_b[j], ssm_dt_bias[j],
                             ssm_A_log[j], ssm_D[j], ssm_norm_w[j], ssm_out_w[j], batch, seq)
        elif kind == 1:
            res = _mla_layer(u, h, next_w, final, mla_in_w[j], mla_q_norm_w[j], mla_kv_norm_w[j], mla_uq_w[j],
                             mla_ukv_w[j], mla_out_w[j], batch, seq)
        elif kind == 2:
            res = _fox_layer(u, h, next_w, final, fox_in_w[j], fox_f_bias[j], fox_out_w[j], batch, seq)
        else:
            res = _dilated_layer(u, h, next_w, final, dil_in_w[j], dil_out_w[j], batch, seq)
        if final:
            return res.reshape(batch, seq, d)
        h, u = res
```

```python
import functools
import math

import jax
import jax.numpy as jnp
from jax import lax
from jax.experimental import pallas as pl
from jax.experimental.pallas import tpu as pltpu

F32 = jnp.float32
BF16 = jnp.bfloat16

D_MODEL = 1024
RMS_EPS = 1e-6
ROPE_THETA = 500000.0
LOG2E = 1.4426950408889634
NEG_BIG = -1e30

LANES = 128

SSM_D_INNER = 2048
SSM_HEAD_DIM = 64
SSM_N_HEADS = 32
SSM_D_STATE = 128
SSM_N_GROUPS = 8
SSM_HEADS_PER_GROUP = SSM_N_HEADS // SSM_N_GROUPS
SSM_GROUP_WIDTH = SSM_HEADS_PER_GROUP * SSM_HEAD_DIM
SSM_CHUNK = 128
SSM_CONV = 4
SSM_BC_DIM = SSM_N_GROUPS * SSM_D_STATE
SSM_MAIN_COLS = 2 * SSM_D_INNER + 2 * SSM_BC_DIM
CONV_HALO = 16
SSD_CHUNKS_PER_STEP = 4

MLA_N_HEADS = 16
MLA_Q_LORA = 384
MLA_KV_LORA = 256
MLA_NOPE_DIM = 128
MLA_ROPE_DIM = 64
MLA_V_DIM = 128
MLA_QK_DIM = MLA_NOPE_DIM + MLA_ROPE_DIM
MLA_WIDTH = MLA_N_HEADS * MLA_V_DIM
MLA_Q_PAD = 2 * LANES
MLA_SMALL_COLS = MLA_Q_LORA + LANES + MLA_KV_LORA

FOX_N_HEADS = 16
FOX_HEAD_DIM = 128
FOX_WIDTH = FOX_N_HEADS * FOX_HEAD_DIM

DIL_CONFIGS = ((128, 1), (512, 4), (2048, 16))
DIL_N_HEADS = 8
DIL_HEAD_DIM = 128
DIL_WIDTH = DIL_N_HEADS * DIL_HEAD_DIM
DIL_ROPE_DIM = DIL_HEAD_DIM // 4

VMEM_LIMIT_BYTES = 56 * 1024 * 1024
PROJ_ROW_TILE = 2048
PROJ_COL_TILE = 1024
OUT_PROJ_ROW_TILE = 1024
ATTN_TILE = 512


def _params(n_axes):
    return pltpu.CompilerParams(dimension_semantics=("arbitrary",) * n_axes,
                                vmem_limit_bytes=VMEM_LIMIT_BYTES)


def _silu(x):
    return x * jax.nn.sigmoid(x)


def _rms_normalize(x, w):
    ms = jnp.mean(x * x, axis=-1, keepdims=True)
    return (x * lax.rsqrt(ms + RMS_EPS)) * w


ROPE_PARTNER_SHIFT = LANES // 2


def _rope_block(x, cos_t, sin_t):
    return x * cos_t + pltpu.roll(x, ROPE_PARTNER_SHIFT, axis=1) * sin_t


def _rope_split_perm(rope_dim, width=LANES):
    half = rope_dim // 2
    first = list(range(half))
    second = list(range(half, rope_dim))
    rest = list(range(rope_dim, width))
    n_fill = ROPE_PARTNER_SHIFT - half
    return jnp.array(first + rest[:n_fill] + second + rest[n_fill:], jnp.int32)


def _rope_tables(seq, rope_dim):
    half = rope_dim // 2
    inv_freq = ROPE_THETA ** (-jnp.arange(half, dtype=F32) / half)
    ang = jnp.arange(seq, dtype=F32)[:, None] * inv_freq[None, :]
    cos, sin = jnp.cos(ang), jnp.sin(ang)
    ones = jnp.ones((seq, ROPE_PARTNER_SHIFT - half), F32)
    zeros = jnp.zeros((seq, ROPE_PARTNER_SHIFT - half), F32)
    cos_t = jnp.concatenate([cos, ones, cos, ones], axis=1)
    sin_t = jnp.concatenate([-sin, zeros, sin, zeros], axis=1)
    return cos_t, sin_t


def _rmsnorm_kernel(x_ref, w_ref, o_ref):
    o_ref[...] = _rms_normalize(x_ref[...], w_ref[...]).astype(o_ref.dtype)


def _rmsnorm(x, w, out_dtype, tm=1024):
    m, d = x.shape
    tm = min(tm, m)
    return pl.pallas_call(
        _rmsnorm_kernel,
        grid=(m // tm,),
        in_specs=[pl.BlockSpec((tm, d), lambda i: (i, 0)), pl.BlockSpec((1, d), lambda i: (0, 0))],
        out_specs=pl.BlockSpec((tm, d), lambda i: (i, 0)),
        out_shape=jax.ShapeDtypeStruct((m, d), out_dtype),
        compiler_params=_params(1),
        name="rmsnorm",
    )(x, w.reshape(1, d))


def _mm_kernel(a_ref, w_ref, o_ref):
    o_ref[...] = jnp.dot(a_ref[...], w_ref[...], preferred_element_type=F32).astype(o_ref.dtype)


def _mm(a, w, out_dtype, tm, tn, name):
    m, k = a.shape
    n = w.shape[1]
    tm = min(tm, m)
    tn = min(tn, n)
    return pl.pallas_call(
        _mm_kernel,
        grid=(m // tm, n // tn),
        in_specs=[pl.BlockSpec((tm, k), lambda i, j: (i, 0)), pl.BlockSpec((k, tn), lambda i, j: (0, j))],
        out_specs=pl.BlockSpec((tm, tn), lambda i, j: (i, j)),
        out_shape=jax.ShapeDtypeStruct((m, n), out_dtype),
        compiler_params=_params(2),
        name=name,
    )(a, w)


def _mm_rope_kernel(a_ref, w_ref, cos_ref, sin_ref, o_ref):
    acc = jnp.dot(a_ref[...], w_ref[...], preferred_element_type=F32)
    cos_t, sin_t = cos_ref[...], sin_ref[...]
    for blk in range(acc.shape[1] // LANES):
        cols = slice(blk * LANES, (blk + 1) * LANES)
        o_ref[:, cols] = _rope_block(acc[:, cols], cos_t, sin_t).astype(o_ref.dtype)


def _mm_rope(a, w, tables, seq, out_dtype, tm, tn, name):
    m, k = a.shape
    n = w.shape[1]
    tm = min(tm, seq)
    row_blocks = seq // tm
    tab_spec = pl.BlockSpec((tm, LANES), lambda i, j: (i % row_blocks, 0))
    return pl.pallas_call(
        _mm_rope_kernel,
        grid=(m // tm, n // tn),
        in_specs=[pl.BlockSpec((tm, k), lambda i, j: (i, 0)), pl.BlockSpec((k, tn), lambda i, j: (0, j)),
                  tab_spec, tab_spec],
        out_specs=pl.BlockSpec((tm, tn), lambda i, j: (i, j)),
        out_shape=jax.ShapeDtypeStruct((m, n), out_dtype),
        compiler_params=_params(2),
        name=name,
    )(a, w, *tables)


def _mla_small_kernel(a_ref, w_ref, qn_ref, kvn_ref, cos_ref, sin_ref, cq_ref, ckv_ref, kr_ref):
    acc = jnp.dot(a_ref[...], w_ref[...], preferred_element_type=F32)
    kv0 = MLA_Q_LORA + LANES
    cq_ref[...] = _rms_normalize(acc[:, :MLA_Q_LORA], qn_ref[...]).astype(cq_ref.dtype)
    ckv_ref[...] = _rms_normalize(acc[:, kv0:kv0 + MLA_KV_LORA], kvn_ref[...]).astype(ckv_ref.dtype)
    kr_ref[...] = _rope_block(acc[:, MLA_Q_LORA:kv0], cos_ref[...], sin_ref[...]).astype(kr_ref.dtype)


def _mla_small(u, w, q_norm_w, kv_norm_w, tables, seq, tm=1024):
    m, k = u.shape
    tm = min(tm, seq)
    row_blocks = seq // tm
    tab_spec = pl.BlockSpec((tm, LANES), lambda i: (i % row_blocks, 0))
    rows = lambda width: pl.BlockSpec((tm, width), lambda i: (i, 0))
    whole = lambda r, c: pl.BlockSpec((r, c), lambda i: (0, 0))
    return pl.pallas_call(
        _mla_small_kernel,
        grid=(m // tm,),
        in_specs=[rows(k), whole(k, MLA_SMALL_COLS), whole(1, MLA_Q_LORA), whole(1, MLA_KV_LORA),
                  tab_spec, tab_spec],
        out_specs=[rows(MLA_Q_LORA), rows(MLA_KV_LORA), rows(LANES)],
        out_shape=[jax.ShapeDtypeStruct((m, MLA_Q_LORA), BF16), jax.ShapeDtypeStruct((m, MLA_KV_LORA), BF16),
                   jax.ShapeDtypeStruct((m, LANES), BF16)],
        compiler_params=_params(1),
        name="mla_in_small",
    )(u, w, q_norm_w.reshape(1, -1), kv_norm_w.reshape(1, -1), *tables)


def _out_proj_kernel(a_ref, w_ref, h_ref, nw_ref, *out_refs, final):
    h_new = h_ref[...] + jnp.dot(a_ref[...], w_ref[...], preferred_element_type=F32)
    if final:
        (u_ref,) = out_refs
    else:
        hn_ref, u_ref = out_refs
        hn_ref[...] = h_new
    u_ref[...] = _rms_normalize(h_new, nw_ref[...]).astype(u_ref.dtype)


def _out_proj(a, w, h, norm_w, final, tm=OUT_PROJ_ROW_TILE):
    m, k = a.shape
    d = w.shape[1]
    tm = min(tm, m)
    row_spec = pl.BlockSpec((tm, d), lambda i: (i, 0))
    if final:
        out_specs, out_shape = row_spec, jax.ShapeDtypeStruct((m, d), F32)
    else:
        out_specs = [row_spec, row_spec]
        out_shape = [jax.ShapeDtypeStruct((m, d), F32), jax.ShapeDtypeStruct((m, d), BF16)]
    return pl.pallas_call(
        functools.partial(_out_proj_kernel, final=final),
        grid=(m // tm,),
        in_specs=[pl.BlockSpec((tm, k), lambda i: (i, 0)), pl.BlockSpec((k, d), lambda i: (0, 0)),
                  row_spec, pl.BlockSpec((1, d), lambda i: (0, 0))],
        out_specs=out_specs,
        out_shape=out_shape,
        compiler_params=_params(1),
        name="out_proj_final" if final else "out_proj",
    )(a, w, h, norm_w.reshape(1, d))


def _expand_heads(v):
    t = v.shape[0]
    cols = [jnp.broadcast_to(v[:, h:h + 1], (t, LANES)) for h in range(SSM_HEADS_PER_GROUP)]
    lane = lax.broadcasted_iota(jnp.int32, (t, LANES), 1)
    first = lane < SSM_HEAD_DIM
    return jnp.concatenate([jnp.where(first, cols[0], cols[1]), jnp.where(first, cols[2], cols[3])], axis=1)


def _cumsum_rows(x):
    n = x.shape[0]
    row = lax.broadcasted_iota(jnp.int32, x.shape, 0)
    shift = 1
    while shift < n:
        x = x + jnp.where(row >= shift, pltpu.roll(x, shift, axis=0), 0.0)
        shift *= 2
    return x


def _ssd_kernel(z_ref, x_ref, b_ref, c_ref, dt_ref, wx_ref, wb_ref, wc_ref, bx_ref, bb_ref, bc_ref,
                dtb_ref, alog_ref, dskip_ref, nw_ref, o_ref, *, seq):
    t = SSM_CHUNK
    n_chunks = seq // t

    gw, ns = SSM_GROUP_WIDTH, SSM_D_STATE
    conv_w = jnp.concatenate([wx_ref[...], wb_ref[...], wc_ref[...]], axis=1)
    conv_b = jnp.concatenate([bx_ref[...], bb_ref[...], bc_ref[...]], axis=1)
    ext_rows = 2 * t
    n_shift = SSM_CONV - 1
    sel_row = lax.broadcasted_iota(jnp.int32, (n_shift * t, ext_rows), 0)
    sel_col = lax.broadcasted_iota(jnp.int32, (n_shift * t, ext_rows), 1)
    shift_select = (sel_col == CONV_HALO + sel_row % t - (n_shift - sel_row // t)).astype(BF16)
    ext_fill = jnp.zeros((ext_rows - t - CONV_HALO, gw + 2 * ns), BF16)

    def conv_shifts(t0, c):
        rows = pl.ds(t0, t)
        cur = jnp.concatenate([x_ref[rows, :], b_ref[rows, :], c_ref[rows, :]], axis=1)
        halo = pl.ds(pl.multiple_of(jnp.maximum(t0 - CONV_HALO, 0), CONV_HALO), CONV_HALO)
        prev = jnp.concatenate([x_ref[halo, :], b_ref[halo, :], c_ref[halo, :]], axis=1)
        prev = jnp.where(c > 0, prev, jnp.zeros_like(prev))
        ext = jnp.concatenate([prev, cur, ext_fill], axis=0)
        return cur, jnp.dot(shift_select, ext, preferred_element_type=F32)

    def conv_silu(cur, shifted):
        acc = conv_b + conv_w[n_shift:n_shift + 1, :] * cur.astype(F32)
        for tap in range(n_shift):
            acc = acc + conv_w[tap:tap + 1, :] * shifted[tap * t:(tap + 1) * t, :]
        act = _silu(acc)
        return act[:, :gw], act[:, gw:gw + ns], act[:, gw + ns:]

    row = lax.broadcasted_iota(jnp.int32, (t, t), 0)
    col = lax.broadcasted_iota(jnp.int32, (t, t), 1)
    causal = row >= col
    head_of_lane = lax.broadcasted_iota(jnp.int32, (t, SSM_GROUP_WIDTH), 1) // SSM_HEAD_DIM
    neg_a = -jnp.exp(alog_ref[...])
    group_lane_shift = (LANES - SSM_HEADS_PER_GROUP * pl.program_id(1)) % LANES

    per_step = SSD_CHUNKS_PER_STEP if n_chunks % SSD_CHUNKS_PER_STEP == 0 else 1

    def chunk_group(i, state):
        chunks = [i * per_step + j for j in range(per_step)]
        starts = [pl.multiple_of(c * t, t) for c in chunks]
        conv_in = [conv_shifts(t0, c) for t0, c in zip(starts, chunks)]

        stage2 = []
        for t0, (cur, shifted) in zip(starts, conv_in):
            x, b_in, c_out = conv_silu(cur, shifted)
            dt_raw = pltpu.roll(dt_ref[pl.ds(t0, t), :], group_lane_shift, axis=1)
            dt = jax.nn.softplus(dt_raw + dtb_ref[...])
            a_cum = _cumsum_rows(dt * neg_a)
            a_cum_wide = _expand_heads(a_cum)
            exp_cum = jnp.exp(a_cum_wide)
            decay_to_end = jnp.exp(a_cum_wide[t - 1:t, :] - a_cum_wide)
            xdt = x * _expand_heads(dt)
            c_bf = c_out.astype(BF16)
            cb = lax.dot_general(c_bf, b_in.astype(BF16), (((1,), (1,)), ((), ())),
                                 preferred_element_type=F32)
            stage2.append((x, b_in, c_bf, a_cum, exp_cum, decay_to_end, xdt, cb))

        stage3 = []
        for x, b_in, c_bf, a_cum, exp_cum, decay_to_end, xdt, cb in stage2:
            a_cum_t = a_cum.T
            y_diag = None
            for h in range(SSM_HEADS_PER_GROUP):
                seg = jnp.broadcast_to(a_cum[:, h:h + 1], (t, t)) - jnp.broadcast_to(a_cum_t[h:h + 1, :], (t, t))
                decay = jnp.exp(jnp.where(causal, seg, NEG_BIG))
                x_h = jnp.where(head_of_lane == h, xdt, 0.0).astype(BF16)
                term = jnp.dot((cb * decay).astype(BF16), x_h, preferred_element_type=F32)
                y_diag = term if y_diag is None else y_diag + term
            stage3.append((x, c_bf, exp_cum, y_diag, b_in.T.astype(BF16), (xdt * decay_to_end).astype(BF16)))

        for t0, (x, c_bf, exp_cum, y_diag, b_t, x_to_end) in zip(starts, stage3):
            y = y_diag + jnp.dot(c_bf, state.astype(BF16), preferred_element_type=F32) * exp_cum
            state = state * exp_cum[t - 1:t, :] + jnp.dot(b_t, x_to_end, preferred_element_type=F32)
            y = y + dskip_ref[...] * x
            g = y * _silu(z_ref[pl.ds(t0, t), :].astype(F32))
            o_ref[pl.ds(t0, t), :] = _rms_normalize(g, nw_ref[...]).astype(o_ref.dtype)
        return state

    lax.fori_loop(0, n_chunks // per_step, chunk_group, jnp.zeros((SSM_D_STATE, SSM_GROUP_WIDTH), F32))


def _ssd_mixer(proj, dt_raw, conv_w, conv_b, dt_bias, a_log, d_skip, norm_w, batch, seq):
    m = proj.shape[0]
    gw, ns = SSM_GROUP_WIDTH, SSM_D_STATE
    x_blk0 = SSM_D_INNER // gw
    b_blk0 = 2 * SSM_D_INNER // ns
    c_blk0 = b_blk0 + SSM_N_GROUPS
    cw_b0 = SSM_D_INNER // ns
    cw_c0 = cw_b0 + SSM_N_GROUPS

    def pad_heads(v):
        out = jnp.zeros((SSM_N_GROUPS, 1, LANES), F32)
        return out.at[:, 0, :SSM_HEADS_PER_GROUP].set(v.astype(F32).reshape(SSM_N_GROUPS, SSM_HEADS_PER_GROUP))

    d_lanes = jnp.repeat(d_skip.astype(F32).reshape(SSM_N_GROUPS, SSM_HEADS_PER_GROUP), SSM_HEAD_DIM,
                         axis=1).reshape(SSM_N_GROUPS, 1, gw)
    conv_b2 = conv_b.reshape(1, -1)
    nw2 = norm_w.reshape(1, -1)
    seq_spec = lambda width, blk0: pl.BlockSpec((seq, width), lambda b, g: (b, blk0 + g))
    row_spec = lambda rows, width, blk0: pl.BlockSpec((rows, width), lambda b, g: (0, blk0 + g))
    grp_spec = lambda width: pl.BlockSpec((None, 1, width), lambda b, g: (g, 0, 0))
    return pl.pallas_call(
        functools.partial(_ssd_kernel, seq=seq),
        grid=(batch, SSM_N_GROUPS),
        in_specs=[seq_spec(gw, 0), seq_spec(gw, x_blk0), seq_spec(ns, b_blk0), seq_spec(ns, c_blk0),
                  pl.BlockSpec((seq, LANES), lambda b, g: (b, 0)),
                  row_spec(SSM_CONV, gw, 0), row_spec(SSM_CONV, ns, cw_b0), row_spec(SSM_CONV, ns, cw_c0),
                  row_spec(1, gw, 0), row_spec(1, ns, cw_b0), row_spec(1, ns, cw_c0),
                  grp_spec(LANES), grp_spec(LANES), grp_spec(gw), row_spec(1, gw, 0)],
        out_specs=pl.BlockSpec((seq, gw), lambda b, g: (b, g)),
        out_shape=jax.ShapeDtypeStruct((m, SSM_D_INNER), BF16),
        compiler_params=_params(2),
        name="ssd_mixer",
    )(proj, proj, proj, proj, dt_raw, conv_w, conv_w, conv_w, conv_b2, conv_b2, conv_b2,
      pad_heads(dt_bias), pad_heads(a_log), d_lanes, nw2)


def _causal_attention(load_q, k_ref, v_ref, z_ref, o_ref, tile, key_bias=None):
    n_tiles = k_ref.shape[0] // tile
    row = lax.broadcasted_iota(jnp.int32, (tile, tile), 0)
    col = lax.broadcasted_iota(jnp.int32, (tile, tile), 1)
    on_or_below_diagonal = row >= col

    def logits(q, k_rows):
        s = lax.dot_general(q, k_ref[k_rows, :], (((1,), (1,)), ((), ())), preferred_element_type=F32)
        if key_bias is not None:
            s = s - key_bias(k_rows)
        return s

    order = list(reversed(range(n_tiles)))
    scores = {}
    for qi in order:
        q_rows = slice(qi * tile, (qi + 1) * tile)
        q = load_q(q_rows)
        scores[qi] = (jnp.where(on_or_below_diagonal, logits(q, q_rows), NEG_BIG),
                      logits(q, slice(0, qi * tile)) if qi > 0 else None)
    for qi in order:
        q_rows = slice(qi * tile, (qi + 1) * tile)
        before = slice(0, qi * tile)
        s_diag, s_before = scores[qi]
        m = jnp.max(s_diag, axis=1, keepdims=True)
        if qi > 0:
            m = jnp.maximum(m, jnp.max(s_before, axis=1, keepdims=True))
        acc = jnp.dot(jnp.exp2(s_diag - m).astype(BF16), v_ref[q_rows, :], preferred_element_type=F32)
        if qi > 0:
            acc = acc + jnp.dot(jnp.exp2(s_before - m).astype(BF16), v_ref[before, :], preferred_element_type=F32)
        dv = acc.shape[1] // 2
        o = acc[:, :dv] * (1.0 / acc[:, dv:])
        o_ref[q_rows, :] = (o * _silu(z_ref[q_rows, :].astype(F32))).astype(o_ref.dtype)


CUM_ROWS_PER_BLOCK = 8


def _fox_attention_kernel(q_ref, k_ref, v_ref, cum_ref, z_ref, o_ref, v_ones_s, *, tile):
    cum_row = pl.program_id(1) % CUM_ROWS_PER_BLOCK
    key_bias = lambda k_rows: cum_ref[pl.ds(cum_row, 1), k_rows]
    v_ones_s[:, :FOX_HEAD_DIM] = v_ref[...]
    v_ones_s[:, FOX_HEAD_DIM:] = jnp.ones(v_ref.shape, BF16)
    _causal_attention(lambda rows: q_ref[rows, :], k_ref, v_ones_s, z_ref, o_ref, tile, key_bias)


def _fox_attention(proj, cum_t, batch, seq, tile=ATTN_TILE):
    m = proj.shape[0]
    hh = FOX_N_HEADS
    head_block = lambda blk0: pl.BlockSpec((seq, LANES), lambda b, h: (b, blk0 + h))
    return pl.pallas_call(
        functools.partial(_fox_attention_kernel, tile=min(tile, seq)),
        grid=(batch, hh),
        in_specs=[head_block(0), head_block(hh), head_block(2 * hh),
                  pl.BlockSpec((None, CUM_ROWS_PER_BLOCK, seq), lambda b, h: (b, h // CUM_ROWS_PER_BLOCK, 0)),
                  head_block(3 * hh)],
        out_specs=pl.BlockSpec((seq, LANES), lambda b, h: (b, h)),
        out_shape=jax.ShapeDtypeStruct((m, FOX_WIDTH), BF16),
        scratch_shapes=[pltpu.VMEM((seq, 2 * FOX_HEAD_DIM), BF16)],
        compiler_params=_params(2),
        name="fox_attention",
    )(proj, proj, proj, cum_t, proj)


def _mla_attention_kernel(cq_ref, ckv_ref, wq_ref, wkv_ref, kr_ref, cos_ref, sin_ref, z_ref, o_ref,
                          q_s, k_s, v_s, *, tile):
    k_s[:, MLA_NOPE_DIM:] = kr_ref[...]
    v_s[:, MLA_V_DIM:] = jnp.ones((v_s.shape[0], MLA_V_DIM), BF16)
    for part in range(q_s.shape[0] // tile):
        rows = slice(part * tile, (part + 1) * tile)
        q = jnp.dot(cq_ref[rows, :], wq_ref[...], preferred_element_type=F32)
        q_s[rows, :MLA_NOPE_DIM] = q[:, :MLA_NOPE_DIM].astype(BF16)
        q_s[rows, MLA_NOPE_DIM:] = _rope_block(q[:, MLA_NOPE_DIM:], cos_ref[rows, :], sin_ref[rows, :]).astype(BF16)
        kv = jnp.dot(ckv_ref[rows, :], wkv_ref[...], preferred_element_type=F32)
        k_s[rows, :MLA_NOPE_DIM] = kv[:, :MLA_NOPE_DIM].astype(BF16)
        v_s[rows, :MLA_V_DIM] = kv[:, MLA_NOPE_DIM:].astype(BF16)
    _causal_attention(lambda rows: q_s[rows, :], k_s, v_s, z_ref, o_ref, tile)


def _mla_attention(cq, ckv, w_q, w_kv, k_rope, tables, z, batch, seq, tile=ATTN_TILE):
    m = cq.shape[0]
    per_batch = lambda width: pl.BlockSpec((seq, width), lambda b, h: (b, 0))
    per_head = lambda rows: pl.BlockSpec((rows, MLA_Q_PAD), lambda b, h: (0, h))
    table = pl.BlockSpec((seq, LANES), lambda b, h: (0, 0))
    return pl.pallas_call(
        functools.partial(_mla_attention_kernel, tile=min(tile, seq)),
        grid=(batch, MLA_N_HEADS),
        in_specs=[per_batch(MLA_Q_LORA), per_batch(MLA_KV_LORA), per_head(MLA_Q_LORA), per_head(MLA_KV_LORA),
                  per_batch(LANES), table, table, pl.BlockSpec((seq, LANES), lambda b, h: (b, h))],
        out_specs=pl.BlockSpec((seq, LANES), lambda b, h: (b, h)),
        out_shape=jax.ShapeDtypeStruct((m, MLA_WIDTH), BF16),
        scratch_shapes=[pltpu.VMEM((seq, MLA_Q_PAD), BF16), pltpu.VMEM((seq, MLA_Q_PAD), BF16),
                        pltpu.VMEM((seq, 2 * MLA_V_DIM), BF16)],
        compiler_params=_params(2),
        name="mla_attention",
    )(cq, ckv, w_q, w_kv, k_rope, *tables, z)


def _forget_cum_kernel(f_ref, b_ref, o_ref):
    x = f_ref[...] + b_ref[...]
    log_f = jnp.minimum(x, 0.0) - jnp.log1p(jnp.exp(-jnp.abs(x)))
    o_ref[...] = (_cumsum_rows(log_f) * LOG2E).T


def _forget_cum(f_raw, f_bias_pad, batch, seq):
    return pl.pallas_call(
        _forget_cum_kernel,
        grid=(batch,),
        in_specs=[pl.BlockSpec((seq, LANES), lambda b: (b, 0)), pl.BlockSpec((1, LANES), lambda b: (0, 0))],
        out_specs=pl.BlockSpec((None, LANES, seq), lambda b: (b, 0, 0)),
        out_shape=jax.ShapeDtypeStruct((batch, LANES, seq), F32),
        compiler_params=_params(1),
        name="forget_cum",
    )(f_raw, f_bias_pad)


def _dilated_kernel(*refs, seq, pre_regrouped):
    n_grp = len(DIL_CONFIGS)
    qkv_refs = refs[:3 * n_grp]
    z_ref, o_ref = refs[3 * n_grp], refs[3 * n_grp + 1]
    scratch = list(refs[3 * n_grp + 2:])
    take = lambda count: [scratch.pop(0) for _ in range(count)]
    out_s = take(n_grp)
    lse_s = take(n_grp)
    stage_s = take(3)
    strided = [g for g, (_, dil) in enumerate(DIL_CONFIGS) if dil > 1]
    regroup_here = [g for g in strided if g not in pre_regrouped]
    qkv_res = {g: take(3) for g in regroup_here}
    for g in pre_regrouped:
        qkv_res[g] = qkv_refs[3 * g:3 * g + 3]
    out_res = {g: take(1)[0] for g in strided}
    lse_res = {g: take(1)[0] for g in strided}

    n_copy = 0
    for g in regroup_here:
        dil = DIL_CONFIGS[g][1]
        length = seq // dil
        for which in range(3):
            stage = stage_s[n_copy % len(stage_s)]
            n_copy += 1
            stage[...] = qkv_refs[3 * g + which][...].astype(F32)
            for residue in range(dil):
                qkv_res[g][which][residue * length:(residue + 1) * length, :] = (
                    stage[pl.ds(residue, length, stride=dil), :].astype(BF16))

    for g, (window, dil) in enumerate(DIL_CONFIGS):
        span = window // dil
        length = seq // dil
        blk = min(span, length)
        n_blk = length // blk
        q_src, k_src, v_src = qkv_res[g] if dil > 1 else qkv_refs[3 * g:3 * g + 3]
        o_dst, l_dst = (out_res[g], lse_res[g]) if dil > 1 else (out_s[g], lse_s[g])
        row1 = lax.broadcasted_iota(jnp.int32, (blk, blk), 0)
        col1 = lax.broadcasted_iota(jnp.int32, (blk, blk), 1)
        band_cur = (row1 >= col1) & (row1 - col1 <= span)
        band_prev = row1 + blk - col1 <= span
        n_batch = dil * n_blk
        as_blocks = lambda ref: ref[...].reshape(n_batch, blk, LANES)
        q3, k3, v3 = as_blocks(q_src), as_blocks(k_src), as_blocks(v_src)
        v3 = jnp.concatenate([v3, jnp.ones(v3.shape, BF16)], axis=2)
        s_cur = jnp.einsum("jqd,jkd->jqk", q3, k3, preferred_element_type=F32)
        s_cur = jnp.where(band_cur[None], s_cur, NEG_BIG)
        m = jnp.max(s_cur, axis=2, keepdims=True)
        if n_blk > 1:
            shift_blocks = lambda x3: jnp.concatenate([x3[:1], x3[:-1]], axis=0)
            k_prev, v_prev = shift_blocks(k3), shift_blocks(v3)
            block_id = lax.broadcasted_iota(jnp.int32, (n_batch, blk, blk), 0)
            s_prev = jnp.einsum("jqd,jkd->jqk", q3, k_prev, preferred_element_type=F32)
            s_prev = jnp.where(band_prev[None], s_prev, NEG_BIG)
            s_prev = jnp.where(block_id % n_blk != 0, s_prev, NEG_BIG)
            m = jnp.maximum(m, jnp.max(s_prev, axis=2, keepdims=True))
        o = jnp.einsum("jqk,jkd->jqd", jnp.exp2(s_cur - m).astype(BF16), v3, preferred_element_type=F32)
        if n_blk > 1:
            o = o + jnp.einsum("jqk,jkd->jqd", jnp.exp2(s_prev - m).astype(BF16), v_prev,
                               preferred_element_type=F32)
        den = o[:, :, LANES:]
        o_dst[...] = (o[:, :, :LANES] * (1.0 / den)).reshape(seq, LANES)
        l_dst[...] = (m + jnp.log2(den)).reshape(seq, LANES)

    for g in strided:
        dil = DIL_CONFIGS[g][1]
        length = seq // dil
        for residue in range(dil):
            src_rows = slice(residue * length, (residue + 1) * length)
            out_s[g][pl.ds(residue, length, stride=dil), :] = out_res[g][src_rows, :]
            lse_s[g][pl.ds(residue, length, stride=dil), :] = lse_res[g][src_rows, :]

    rows_per_step = min(256, seq)
    for i in range(seq // rows_per_step):
        rs = slice(i * rows_per_step, (i + 1) * rows_per_step)
        lses = [lse_s[g][rs, :] for g in range(n_grp)]
        lse_max = functools.reduce(jnp.maximum, lses)
        weights = [jnp.exp2(l - lse_max) for l in lses]
        num = sum(w * out_s[g][rs, :] for g, w in enumerate(weights))
        o = num * (1.0 / sum(weights))
        o_ref[rs, :] = (o * _silu(z_ref[rs, :].astype(F32))).astype(o_ref.dtype)


def _dilated_mixer(qkv_operands, z_operand, pre_regrouped, batch, seq):
    m = z_operand[0].shape[0]
    n_grp = len(DIL_CONFIGS)
    n_strided = sum(dil > 1 for _, dil in DIL_CONFIGS)
    n_regroup = n_strided - len(pre_regrouped)
    head_block = lambda blk0: pl.BlockSpec((seq, LANES), lambda b, h: (b, blk0 + h))
    operands = list(qkv_operands) + [z_operand]
    in_specs = [head_block(blk0) for _, blk0 in operands]
    args = [arr for arr, _ in operands]
    return pl.pallas_call(
        functools.partial(_dilated_kernel, seq=seq, pre_regrouped=tuple(pre_regrouped)),
        grid=(batch, DIL_N_HEADS),
        in_specs=in_specs,
        out_specs=pl.BlockSpec((seq, LANES), lambda b, h: (b, h)),
        out_shape=jax.ShapeDtypeStruct((m, DIL_WIDTH), BF16),
        scratch_shapes=([pltpu.VMEM((seq, LANES), F32)] * (2 * n_grp + 3)
                        + [pltpu.VMEM((seq, LANES), BF16)] * 3 * n_regroup
                        + [pltpu.VMEM((seq, LANES), F32)] * 2 * n_strided),
        compiler_params=_params(2),
        name="dilated_mixer",
    )(*args)


def _ssd_layer(u, h, next_norm_w, final, in_w, conv_w, conv_b, dt_bias, a_log, d_skip, norm_w, out_w,
               batch, seq):
    w_main = in_w[:, :SSM_MAIN_COLS].astype(BF16)
    w_dt = jnp.pad(in_w[:, SSM_MAIN_COLS:], ((0, 0), (0, LANES - SSM_N_HEADS))).astype(BF16)
    proj = _mm(u, w_main, BF16, PROJ_ROW_TILE, PROJ_COL_TILE, "ssd_in_proj")
    dt_raw = _mm(u, w_dt, F32, PROJ_ROW_TILE, PROJ_COL_TILE, "ssd_dt_proj")
    y = _ssd_mixer(proj, dt_raw, conv_w, conv_b, dt_bias, a_log, d_skip, norm_w, batch, seq)
    return _out_proj(y, out_w.astype(BF16), h, next_norm_w, final)


def _mla_layer(u, h, next_norm_w, final, in_w, q_norm_w, kv_norm_w, uq_w, ukv_w, out_w, batch, seq):
    o1 = MLA_Q_LORA
    o2 = o1 + MLA_KV_LORA
    o3 = o2 + MLA_ROPE_DIM
    rope_perm = _rope_split_perm(MLA_ROPE_DIM)
    w_kr = jnp.pad(in_w[:, o2:o3], ((0, 0), (0, LANES - MLA_ROPE_DIM)))[:, rope_perm]
    w_small = jnp.concatenate([in_w[:, :o1], w_kr, in_w[:, o1:o2]], axis=1).astype(BF16)
    w_z = in_w[:, o3:].astype(BF16)
    q_scale = MLA_QK_DIM ** -0.5 * LOG2E
    w_q = (uq_w * q_scale).reshape(MLA_Q_LORA, MLA_N_HEADS, MLA_QK_DIM)
    w_q_rope = jnp.pad(w_q[:, :, MLA_NOPE_DIM:], ((0, 0), (0, 0), (0, LANES - MLA_ROPE_DIM)))[:, :, rope_perm]
    w_q = jnp.concatenate([w_q[:, :, :MLA_NOPE_DIM], w_q_rope], axis=2).reshape(MLA_Q_LORA, -1).astype(BF16)
    w_kv = ukv_w.astype(BF16)
    tables = _rope_tables(seq, MLA_ROPE_DIM)

    cq, ckv, k_rope = _mla_small(u, w_small, q_norm_w, kv_norm_w, tables, seq)
    z = _mm(u, w_z, BF16, PROJ_ROW_TILE, PROJ_COL_TILE, "mla_gate_proj")
    o = _mla_attention(cq, ckv, w_q, w_kv, k_rope, tables, z, batch, seq)
    return _out_proj(o, out_w.astype(BF16), h, next_norm_w, final)


def _fox_layer(u, h, next_norm_w, final, in_w, f_bias, out_w, batch, seq):
    w = FOX_WIDTH
    q_scale = FOX_HEAD_DIM ** -0.5 * LOG2E
    w_main = jnp.concatenate([in_w[:, :w] * q_scale, in_w[:, w:3 * w], in_w[:, 3 * w + FOX_N_HEADS:]],
                             axis=1).astype(BF16)
    w_f = jnp.pad(in_w[:, 3 * w:3 * w + FOX_N_HEADS], ((0, 0), (0, LANES - FOX_N_HEADS))).astype(BF16)
    f_bias_pad = jnp.pad(f_bias.astype(F32), (0, LANES - FOX_N_HEADS)).reshape(1, LANES)
    proj = _mm(u, w_main, BF16, PROJ_ROW_TILE, PROJ_COL_TILE, "fox_in_proj")
    f_raw = _mm(u, w_f, F32, PROJ_ROW_TILE, LANES, "fox_forget_proj")
    cum_t = _forget_cum(f_raw, f_bias_pad, batch, seq)
    o = _fox_attention(proj, cum_t, batch, seq)
    return _out_proj(o, out_w.astype(BF16), h, next_norm_w, final)


def _dilated_layer(u, h, next_norm_w, final, in_w, out_w, batch, seq):
    w = DIL_WIDTH
    q_scale = DIL_HEAD_DIM ** -0.5 * LOG2E
    rope_perm = _rope_split_perm(DIL_ROPE_DIM)

    def split_rotary(cols):
        return cols.reshape(D_MODEL, DIL_N_HEADS, DIL_HEAD_DIM)[:, :, rope_perm].reshape(D_MODEL, w)

    n_grp = len(DIL_CONFIGS)
    qk_cols, v_cols = [], []
    for gi in range(n_grp):
        base = 3 * w * gi
        qk_cols += [split_rotary(in_w[:, base:base + w] * q_scale), split_rotary(in_w[:, base + w:base + 2 * w])]
        v_cols.append(in_w[:, base + 2 * w:base + 3 * w])
    w_gate = in_w[:, 3 * w * n_grp:]
    tables = _rope_tables(seq, DIL_ROPE_DIM)

    last = n_grp - 1
    dil = DIL_CONFIGS[last][1]

    def residue_major(a, lead):
        cols = a.shape[-1]
        return a.reshape(lead, seq // dil, dil, cols).transpose(0, 2, 1, 3).reshape(lead * seq, cols)

    u_res = residue_major(u, batch)
    tables_res = tuple(residue_major(t, 1) for t in tables)
    w_qk = jnp.concatenate(qk_cols[:2 * last], axis=1).astype(BF16)
    w_qk_last = jnp.concatenate(qk_cols[2 * last:], axis=1).astype(BF16)
    w_vz = jnp.concatenate(v_cols[:last] + [w_gate], axis=1).astype(BF16)
    qk = _mm_rope(u, w_qk, tables, seq, BF16, PROJ_ROW_TILE, PROJ_COL_TILE, "dilated_qk_proj")
    qk_last = _mm_rope(u_res, w_qk_last, tables_res, seq, BF16, PROJ_ROW_TILE, PROJ_COL_TILE,
                       "dilated_qk_proj_regrouped")
    vz = _mm(u, w_vz, BF16, PROJ_ROW_TILE, PROJ_COL_TILE, "dilated_vz_proj")
    v_last = _mm(u_res, v_cols[last].astype(BF16), BF16, PROJ_ROW_TILE, PROJ_COL_TILE, "dilated_v_proj_regrouped")
    hb = DIL_N_HEADS
    operands = []
    for gi in range(last):
        operands += [(qk, 2 * hb * gi), (qk, 2 * hb * gi + hb), (vz, hb * gi)]
    operands += [(qk_last, 0), (qk_last, hb), (v_last, 0)]
    o = _dilated_mixer(operands, (vz, hb * last), (last,), batch, seq)
    return _out_proj(o, out_w.astype(BF16), h, next_norm_w, final)


def kernel(x, norm_w, final_norm_w, ssm_in_w, ssm_conv_w, ssm_conv_b, ssm_dt_bias, ssm_A_log, ssm_D,
           ssm_norm_w, ssm_out_w, mla_in_w, mla_q_norm_w, mla_kv_norm_w, mla_uq_w, mla_ukv_w, mla_out_w,
           fox_in_w, fox_f_bias, fox_out_w, dil_in_w, dil_out_w):
    batch, seq, d = x.shape
    depth = norm_w.shape[0]
    h = x.reshape(batch * seq, d)
    u = _rmsnorm(h, norm_w[0], BF16)
    for i in range(depth):
        kind, j = i % 4, i // 4
        final = i == depth - 1
        next_w = final_norm_w if final else norm_w[i + 1]
        if kind == 0:
            res = _ssd_layer(u, h, next_w, final, ssm_in_w[j], ssm_conv_w[j], ssm_conv_b[j], ssm_dt_bias[j],
                             ssm_A_log[j], ssm_D[j], ssm_norm_w[j], ssm_out_w[j], batch, seq)
        elif kind == 1:
            res = _mla_layer(u, h, next_w, final, mla_in_w[j], mla_q_norm_w[j], mla_kv_norm_w[j], mla_uq_w[j],
                             mla_ukv_w[j], mla_out_w[j], batch, seq)
        elif kind == 2:
            res = _fox_layer(u, h, next_w, final, fox_in_w[j], fox_f_bias[j], fox_out_w[j], batch, seq)
        else:
            res = _dilated_layer(u, h, next_w, final, dil_in_w[j], dil_out_w[j], batch, seq)
        if final:
            return res.reshape(batch, seq, d)
        h, u = res
```

```python
import functools
import math

import jax
import jax.numpy as jnp
from jax import lax
from jax.experimental import pallas as pl
from jax.experimental.pallas import tpu as pltpu

F32 = jnp.float32
BF16 = jnp.bfloat16

D_MODEL = 1024
RMS_EPS = 1e-6
ROPE_THETA = 500000.0
LOG2E = 1.4426950408889634
NEG_BIG = -1e30

LANES = 128

SSM_D_INNER = 2048
SSM_HEAD_DIM = 64
SSM_N_HEADS = 32
SSM_D_STATE = 128
SSM_N_GROUPS = 8
SSM_HEADS_PER_GROUP = SSM_N_HEADS // SSM_N_GROUPS
SSM_GROUP_WIDTH = SSM_HEADS_PER_GROUP * SSM_HEAD_DIM
SSM_CHUNK = 128
SSM_CONV = 4
SSM_BC_DIM = SSM_N_GROUPS * SSM_D_STATE
SSM_MAIN_COLS = 2 * SSM_D_INNER + 2 * SSM_BC_DIM
CONV_HALO = 16
SSD_CHUNKS_PER_STEP = 4

MLA_N_HEADS = 16
MLA_Q_LORA = 384
MLA_KV_LORA = 256
MLA_NOPE_DIM = 128
MLA_ROPE_DIM = 64
MLA_V_DIM = 128
MLA_QK_DIM = MLA_NOPE_DIM + MLA_ROPE_DIM
MLA_WIDTH = MLA_N_HEADS * MLA_V_DIM
MLA_Q_PAD = 2 * LANES
MLA_SMALL_COLS = MLA_Q_LORA + LANES + MLA_KV_LORA

FOX_N_HEADS = 16
FOX_HEAD_DIM = 128
FOX_WIDTH = FOX_N_HEADS * FOX_HEAD_DIM

DIL_CONFIGS = ((128, 1), (512, 4), (2048, 16))
DIL_N_HEADS = 8
DIL_HEAD_DIM = 128
DIL_WIDTH = DIL_N_HEADS * DIL_HEAD_DIM
DIL_ROPE_DIM = DIL_HEAD_DIM // 4

VMEM_LIMIT_BYTES = 56 * 1024 * 1024
PROJ_ROW_TILE = 2048
PROJ_COL_TILE = 1024
OUT_PROJ_ROW_TILE = 1024
ATTN_TILE = 256


def _params(n_axes):
    return pltpu.CompilerParams(dimension_semantics=("arbitrary",) * n_axes,
                                vmem_limit_bytes=VMEM_LIMIT_BYTES)


def _silu(x):
    return x * jax.nn.sigmoid(x)


def _rms_normalize(x, w):
    ms = jnp.mean(x * x, axis=-1, keepdims=True)
    return (x * lax.rsqrt(ms + RMS_EPS)) * w


ROPE_PARTNER_SHIFT = LANES // 2


def _rope_block(x, cos_t, sin_t):
    return x * cos_t + pltpu.roll(x, ROPE_PARTNER_SHIFT, axis=1) * sin_t


def _rope_split_perm(rope_dim, width=LANES):
    half = rope_dim // 2
    first = list(range(half))
    second = list(range(half, rope_dim))
    rest = list(range(rope_dim, width))
    n_fill = ROPE_PARTNER_SHIFT - half
    return jnp.array(first + rest[:n_fill] + second + rest[n_fill:], jnp.int32)


def _rope_tables(seq, rope_dim):
    half = rope_dim // 2
    inv_freq = ROPE_THETA ** (-jnp.arange(half, dtype=F32) / half)
    ang = jnp.arange(seq, dtype=F32)[:, None] * inv_freq[None, :]
    cos, sin = jnp.cos(ang), jnp.sin(ang)
    ones = jnp.ones((seq, ROPE_PARTNER_SHIFT - half), F32)
    zeros = jnp.zeros((seq, ROPE_PARTNER_SHIFT - half), F32)
    cos_t = jnp.concatenate([cos, ones, cos, ones], axis=1)
    sin_t = jnp.concatenate([-sin, zeros, sin, zeros], axis=1)
    return cos_t, sin_t


def _rmsnorm_kernel(x_ref, w_ref, o_ref):
    o_ref[...] = _rms_normalize(x_ref[...], w_ref[...]).astype(o_ref.dtype)


def _rmsnorm(x, w, out_dtype, tm=1024):
    m, d = x.shape
    tm = min(tm, m)
    return pl.pallas_call(
        _rmsnorm_kernel,
        grid=(m // tm,),
        in_specs=[pl.BlockSpec((tm, d), lambda i: (i, 0)), pl.BlockSpec((1, d), lambda i: (0, 0))],
        out_specs=pl.BlockSpec((tm, d), lambda i: (i, 0)),
        out_shape=jax.ShapeDtypeStruct((m, d), out_dtype),
        compiler_params=_params(1),
        name="rmsnorm",
    )(x, w.reshape(1, d))


def _mm_kernel(a_ref, w_ref, o_ref):
    o_ref[...] = jnp.dot(a_ref[...], w_ref[...], preferred_element_type=F32).astype(o_ref.dtype)


def _mm(a, w, out_dtype, tm, tn, name):
    m, k = a.shape
    n = w.shape[1]
    tm = min(tm, m)
    tn = min(tn, n)
    return pl.pallas_call(
        _mm_kernel,
        grid=(m // tm, n // tn),
        in_specs=[pl.BlockSpec((tm, k), lambda i, j: (i, 0)), pl.BlockSpec((k, tn), lambda i, j: (0, j))],
        out_specs=pl.BlockSpec((tm, tn), lambda i, j: (i, j)),
        out_shape=jax.ShapeDtypeStruct((m, n), out_dtype),
        compiler_params=_params(2),
        name=name,
    )(a, w)


def _mm_rope_kernel(a_ref, w_ref, cos_ref, sin_ref, o_ref):
    acc = jnp.dot(a_ref[...], w_ref[...], preferred_element_type=F32)
    cos_t, sin_t = cos_ref[...], sin_ref[...]
    for blk in range(acc.shape[1] // LANES):
        cols = slice(blk * LANES, (blk + 1) * LANES)
        o_ref[:, cols] = _rope_block(acc[:, cols], cos_t, sin_t).astype(o_ref.dtype)


def _mm_rope(a, w, tables, seq, out_dtype, tm, tn, name):
    m, k = a.shape
    n = w.shape[1]
    tm = min(tm, seq)
    row_blocks = seq // tm
    tab_spec = pl.BlockSpec((tm, LANES), lambda i, j: (i % row_blocks, 0))
    return pl.pallas_call(
        _mm_rope_kernel,
        grid=(m // tm, n // tn),
        in_specs=[pl.BlockSpec((tm, k), lambda i, j: (i, 0)), pl.BlockSpec((k, tn), lambda i, j: (0, j)),
                  tab_spec, tab_spec],
        out_specs=pl.BlockSpec((tm, tn), lambda i, j: (i, j)),
        out_shape=jax.ShapeDtypeStruct((m, n), out_dtype),
        compiler_params=_params(2),
        name=name,
    )(a, w, *tables)


def _mla_small_kernel(a_ref, w_ref, qn_ref, kvn_ref, cos_ref, sin_ref, cq_ref, ckv_ref, kr_ref):
    acc = jnp.dot(a_ref[...], w_ref[...], preferred_element_type=F32)
    kv0 = MLA_Q_LORA + LANES
    cq_ref[...] = _rms_normalize(acc[:, :MLA_Q_LORA], qn_ref[...]).astype(cq_ref.dtype)
    ckv_ref[...] = _rms_normalize(acc[:, kv0:kv0 + MLA_KV_LORA], kvn_ref[...]).astype(ckv_ref.dtype)
    kr_ref[...] = _rope_block(acc[:, MLA_Q_LORA:kv0], cos_ref[...], sin_ref[...]).astype(kr_ref.dtype)


def _mla_small(u, w, q_norm_w, kv_norm_w, tables, seq, tm=1024):
    m, k = u.shape
    tm = min(tm, seq)
    row_blocks = seq // tm
    tab_spec = pl.BlockSpec((tm, LANES), lambda i: (i % row_blocks, 0))
    rows = lambda width: pl.BlockSpec((tm, width), lambda i: (i, 0))
    whole = lambda r, c: pl.BlockSpec((r, c), lambda i: (0, 0))
    return pl.pallas_call(
        _mla_small_kernel,
        grid=(m // tm,),
        in_specs=[rows(k), whole(k, MLA_SMALL_COLS), whole(1, MLA_Q_LORA), whole(1, MLA_KV_LORA),
                  tab_spec, tab_spec],
        out_specs=[rows(MLA_Q_LORA), rows(MLA_KV_LORA), rows(LANES)],
        out_shape=[jax.ShapeDtypeStruct((m, MLA_Q_LORA), BF16), jax.ShapeDtypeStruct((m, MLA_KV_LORA), BF16),
                   jax.ShapeDtypeStruct((m, LANES), BF16)],
        compiler_params=_params(1),
        name="mla_in_small",
    )(u, w, q_norm_w.reshape(1, -1), kv_norm_w.reshape(1, -1), *tables)


def _out_proj_kernel(a_ref, w_ref, h_ref, nw_ref, *out_refs, final):
    h_new = h_ref[...] + jnp.dot(a_ref[...], w_ref[...], preferred_element_type=F32)
    if final:
        (u_ref,) = out_refs
    else:
        hn_ref, u_ref = out_refs
        hn_ref[...] = h_new
    u_ref[...] = _rms_normalize(h_new, nw_ref[...]).astype(u_ref.dtype)


def _out_proj(a, w, h, norm_w, final, tm=OUT_PROJ_ROW_TILE):
    m, k = a.shape
    d = w.shape[1]
    tm = min(tm, m)
    row_spec = pl.BlockSpec((tm, d), lambda i: (i, 0))
    if final:
        out_specs, out_shape = row_spec, jax.ShapeDtypeStruct((m, d), F32)
    else:
        out_specs = [row_spec, row_spec]
        out_shape = [jax.ShapeDtypeStruct((m, d), F32), jax.ShapeDtypeStruct((m, d), BF16)]
    return pl.pallas_call(
        functools.partial(_out_proj_kernel, final=final),
        grid=(m // tm,),
        in_specs=[pl.BlockSpec((tm, k), lambda i: (i, 0)), pl.BlockSpec((k, d), lambda i: (0, 0)),
                  row_spec, pl.BlockSpec((1, d), lambda i: (0, 0))],
        out_specs=out_specs,
        out_shape=out_shape,
        compiler_params=_params(1),
        name="out_proj_final" if final else "out_proj",
    )(a, w, h, norm_w.reshape(1, d))


def _expand_heads(v):
    t = v.shape[0]
    cols = [jnp.broadcast_to(v[:, h:h + 1], (t, LANES)) for h in range(SSM_HEADS_PER_GROUP)]
    lane = lax.broadcasted_iota(jnp.int32, (t, LANES), 1)
    first = lane < SSM_HEAD_DIM
    return jnp.concatenate([jnp.where(first, cols[0], cols[1]), jnp.where(first, cols[2], cols[3])], axis=1)


def _cumsum_rows(x):
    n = x.shape[0]
    row = lax.broadcasted_iota(jnp.int32, x.shape, 0)
    shift = 1
    while shift < n:
        x = x + jnp.where(row >= shift, pltpu.roll(x, shift, axis=0), 0.0)
        shift *= 2
    return x


def _ssd_kernel(z_ref, x_ref, b_ref, c_ref, dt_ref, wx_ref, wb_ref, wc_ref, bx_ref, bb_ref, bc_ref,
                dtb_ref, alog_ref, dskip_ref, nw_ref, o_ref, *, seq):
    t = SSM_CHUNK
    n_chunks = seq // t

    gw, ns = SSM_GROUP_WIDTH, SSM_D_STATE
    conv_w = jnp.concatenate([wx_ref[...], wb_ref[...], wc_ref[...]], axis=1)
    conv_b = jnp.concatenate([bx_ref[...], bb_ref[...], bc_ref[...]], axis=1)
    ext_rows = 2 * t
    n_shift = SSM_CONV - 1
    sel_row = lax.broadcasted_iota(jnp.int32, (n_shift * t, ext_rows), 0)
    sel_col = lax.broadcasted_iota(jnp.int32, (n_shift * t, ext_rows), 1)
    shift_select = (sel_col == CONV_HALO + sel_row % t - (n_shift - sel_row // t)).astype(BF16)
    ext_fill = jnp.zeros((ext_rows - t - CONV_HALO, gw + 2 * ns), BF16)

    def conv_shifts(t0, c):
        rows = pl.ds(t0, t)
        cur = jnp.concatenate([x_ref[rows, :], b_ref[rows, :], c_ref[rows, :]], axis=1)
        halo = pl.ds(pl.multiple_of(jnp.maximum(t0 - CONV_HALO, 0), CONV_HALO), CONV_HALO)
        prev = jnp.concatenate([x_ref[halo, :], b_ref[halo, :], c_ref[halo, :]], axis=1)
        prev = jnp.where(c > 0, prev, jnp.zeros_like(prev))
        ext = jnp.concatenate([prev, cur, ext_fill], axis=0)
        return cur, jnp.dot(shift_select, ext, preferred_element_type=F32)

    def conv_silu(cur, shifted):
        acc = conv_b + conv_w[n_shift:n_shift + 1, :] * cur.astype(F32)
        for tap in range(n_shift):
            acc = acc + conv_w[tap:tap + 1, :] * shifted[tap * t:(tap + 1) * t, :]
        act = _silu(acc)
        return act[:, :gw], act[:, gw:gw + ns], act[:, gw + ns:]

    row = lax.broadcasted_iota(jnp.int32, (t, t), 0)
    col = lax.broadcasted_iota(jnp.int32, (t, t), 1)
    causal = row >= col
    head_of_lane = lax.broadcasted_iota(jnp.int32, (t, SSM_GROUP_WIDTH), 1) // SSM_HEAD_DIM
    neg_a = -jnp.exp(alog_ref[...])
    group_lane_shift = (LANES - SSM_HEADS_PER_GROUP * pl.program_id(1)) % LANES

    per_step = SSD_CHUNKS_PER_STEP if n_chunks % SSD_CHUNKS_PER_STEP == 0 else 1

    def chunk_group(i, state):
        chunks = [i * per_step + j for j in range(per_step)]
        starts = [pl.multiple_of(c * t, t) for c in chunks]
        conv_in = [conv_shifts(t0, c) for t0, c in zip(starts, chunks)]

        stage2 = []
        for t0, (cur, shifted) in zip(starts, conv_in):
            x, b_in, c_out = conv_silu(cur, shifted)
            dt_raw = pltpu.roll(dt_ref[pl.ds(t0, t), :], group_lane_shift, axis=1)
            dt = jax.nn.softplus(dt_raw + dtb_ref[...])
            a_cum = _cumsum_rows(dt * neg_a)
            a_cum_wide = _expand_heads(a_cum)
            exp_cum = jnp.exp(a_cum_wide)
            decay_to_end = jnp.exp(a_cum_wide[t - 1:t, :] - a_cum_wide)
            xdt = x * _expand_heads(dt)
            c_bf = c_out.astype(BF16)
            cb = lax.dot_general(c_bf, b_in.astype(BF16), (((1,), (1,)), ((), ())),
                                 preferred_element_type=F32)
            stage2.append((x, b_in, c_bf, a_cum, exp_cum, decay_to_end, xdt, cb))

        stage3 = []
        for x, b_in, c_bf, a_cum, exp_cum, decay_to_end, xdt, cb in stage2:
            a_cum_t = a_cum.T
            y_diag = None
            for h in range(SSM_HEADS_PER_GROUP):
                seg = jnp.broadcast_to(a_cum[:, h:h + 1], (t, t)) - jnp.broadcast_to(a_cum_t[h:h + 1, :], (t, t))
                decay = jnp.exp(jnp.where(causal, seg, NEG_BIG))
                x_h = jnp.where(head_of_lane == h, xdt, 0.0).astype(BF16)
                term = jnp.dot((cb * decay).astype(BF16), x_h, preferred_element_type=F32)
                y_diag = term if y_diag is None else y_diag + term
            stage3.append((x, c_bf, exp_cum, y_diag, b_in.T.astype(BF16), (xdt * decay_to_end).astype(BF16)))

        for t0, (x, c_bf, exp_cum, y_diag, b_t, x_to_end) in zip(starts, stage3):
            y = y_diag + jnp.dot(c_bf, state.astype(BF16), preferred_element_type=F32) * exp_cum
            state = state * exp_cum[t - 1:t, :] + jnp.dot(b_t, x_to_end, preferred_element_type=F32)
            y = y + dskip_ref[...] * x
            g = y * _silu(z_ref[pl.ds(t0, t), :].astype(F32))
            o_ref[pl.ds(t0, t), :] = _rms_normalize(g, nw_ref[...]).astype(o_ref.dtype)
        return state

    lax.fori_loop(0, n_chunks // per_step, chunk_group, jnp.zeros((SSM_D_STATE, SSM_GROUP_WIDTH), F32))


def _ssd_mixer(proj, dt_raw, conv_w, conv_b, dt_bias, a_log, d_skip, norm_w, batch, seq):
    m = proj.shape[0]
    gw, ns = SSM_GROUP_WIDTH, SSM_D_STATE
    x_blk0 = SSM_D_INNER // gw
    b_blk0 = 2 * SSM_D_INNER // ns
    c_blk0 = b_blk0 + SSM_N_GROUPS
    cw_b0 = SSM_D_INNER // ns
    cw_c0 = cw_b0 + SSM_N_GROUPS

    def pad_heads(v):
        out = jnp.zeros((SSM_N_GROUPS, 1, LANES), F32)
        return out.at[:, 0, :SSM_HEADS_PER_GROUP].set(v.astype(F32).reshape(SSM_N_GROUPS, SSM_HEADS_PER_GROUP))

    d_lanes = jnp.repeat(d_skip.astype(F32).reshape(SSM_N_GROUPS, SSM_HEADS_PER_GROUP), SSM_HEAD_DIM,
                         axis=1).reshape(SSM_N_GROUPS, 1, gw)
    conv_b2 = conv_b.reshape(1, -1)
    nw2 = norm_w.reshape(1, -1)
    seq_spec = lambda width, blk0: pl.BlockSpec((seq, width), lambda b, g: (b, blk0 + g))
    row_spec = lambda rows, width, blk0: pl.BlockSpec((rows, width), lambda b, g: (0, blk0 + g))
    grp_spec = lambda width: pl.BlockSpec((None, 1, width), lambda b, g: (g, 0, 0))
    return pl.pallas_call(
        functools.partial(_ssd_kernel, seq=seq),
        grid=(batch, SSM_N_GROUPS),
        in_specs=[seq_spec(gw, 0), seq_spec(gw, x_blk0), seq_spec(ns, b_blk0), seq_spec(ns, c_blk0),
                  pl.BlockSpec((seq, LANES), lambda b, g: (b, 0)),
                  row_spec(SSM_CONV, gw, 0), row_spec(SSM_CONV, ns, cw_b0), row_spec(SSM_CONV, ns, cw_c0),
                  row_spec(1, gw, 0), row_spec(1, ns, cw_b0), row_spec(1, ns, cw_c0),
                  grp_spec(LANES), grp_spec(LANES), grp_spec(gw), row_spec(1, gw, 0)],
        out_specs=pl.BlockSpec((seq, gw), lambda b, g: (b, g)),
        out_shape=jax.ShapeDtypeStruct((m, SSM_D_INNER), BF16),
        compiler_params=_params(2),
        name="ssd_mixer",
    )(proj, proj, proj, proj, dt_raw, conv_w, conv_w, conv_w, conv_b2, conv_b2, conv_b2,
      pad_heads(dt_bias), pad_heads(a_log), d_lanes, nw2)


def _causal_attention(load_q, k_ref, v_ref, z_ref, o_ref, tile, key_bias=None):
    n_tiles = k_ref.shape[0] // tile
    row = lax.broadcasted_iota(jnp.int32, (tile, tile), 0)
    col = lax.broadcasted_iota(jnp.int32, (tile, tile), 1)
    on_or_below_diagonal = row >= col

    def logits(q, k_rows):
        s = lax.dot_general(q, k_ref[k_rows, :], (((1,), (1,)), ((), ())), preferred_element_type=F32)
        if key_bias is not None:
            s = s - key_bias(k_rows)
        return s

    order = list(reversed(range(n_tiles)))
    scores = {}
    for qi in order:
        q_rows = slice(qi * tile, (qi + 1) * tile)
        q = load_q(q_rows)
        scores[qi] = (jnp.where(on_or_below_diagonal, logits(q, q_rows), NEG_BIG),
                      logits(q, slice(0, qi * tile)) if qi > 0 else None)
    for qi in order:
        q_rows = slice(qi * tile, (qi + 1) * tile)
        before = slice(0, qi * tile)
        s_diag, s_before = scores[qi]
        m = jnp.max(s_diag, axis=1, keepdims=True)
        if qi > 0:
            m = jnp.maximum(m, jnp.max(s_before, axis=1, keepdims=True))
        acc = jnp.dot(jnp.exp2(s_diag - m).astype(BF16), v_ref[q_rows, :], preferred_element_type=F32)
        if qi > 0:
            acc = acc + jnp.dot(jnp.exp2(s_before - m).astype(BF16), v_ref[before, :], preferred_element_type=F32)
        dv = acc.shape[1] // 2
        o = acc[:, :dv] * (1.0 / acc[:, dv:])
        o_ref[q_rows, :] = (o * _silu(z_ref[q_rows, :].astype(F32))).astype(o_ref.dtype)


CUM_ROWS_PER_BLOCK = 8


def _fox_attention_kernel(q_ref, k_ref, v_ref, cum_ref, z_ref, o_ref, v_ones_s, *, tile):
    cum_row = pl.program_id(1) % CUM_ROWS_PER_BLOCK
    key_bias = lambda k_rows: cum_ref[pl.ds(cum_row, 1), k_rows]
    v_ones_s[:, :FOX_HEAD_DIM] = v_ref[...]
    v_ones_s[:, FOX_HEAD_DIM:] = jnp.ones(v_ref.shape, BF16)
    _causal_attention(lambda rows: q_ref[rows, :], k_ref, v_ones_s, z_ref, o_ref, tile, key_bias)


def _fox_attention(proj, cum_t, batch, seq, tile=ATTN_TILE):
    m = proj.shape[0]
    hh = FOX_N_HEADS
    head_block = lambda blk0: pl.BlockSpec((seq, LANES), lambda b, h: (b, blk0 + h))
    return pl.pallas_call(
        functools.partial(_fox_attention_kernel, tile=min(tile, seq)),
        grid=(batch, hh),
        in_specs=[head_block(0), head_block(hh), head_block(2 * hh),
                  pl.BlockSpec((None, CUM_ROWS_PER_BLOCK, seq), lambda b, h: (b, h // CUM_ROWS_PER_BLOCK, 0)),
                  head_block(3 * hh)],
        out_specs=pl.BlockSpec((seq, LANES), lambda b, h: (b, h)),
        out_shape=jax.ShapeDtypeStruct((m, FOX_WIDTH), BF16),
        scratch_shapes=[pltpu.VMEM((seq, 2 * FOX_HEAD_DIM), BF16)],
        compiler_params=_params(2),
        name="fox_attention",
    )(proj, proj, proj, cum_t, proj)


def _mla_attention_kernel(cq_ref, ckv_ref, wq_ref, wkv_ref, kr_ref, cos_ref, sin_ref, z_ref, o_ref,
                          q_s, k_s, v_s, *, tile):
    k_s[:, MLA_NOPE_DIM:] = kr_ref[...]
    v_s[:, MLA_V_DIM:] = jnp.ones((v_s.shape[0], MLA_V_DIM), BF16)
    for part in range(q_s.shape[0] // tile):
        rows = slice(part * tile, (part + 1) * tile)
        q = jnp.dot(cq_ref[rows, :], wq_ref[...], preferred_element_type=F32)
        q_s[rows, :MLA_NOPE_DIM] = q[:, :MLA_NOPE_DIM].astype(BF16)
        q_s[rows, MLA_NOPE_DIM:] = _rope_block(q[:, MLA_NOPE_DIM:], cos_ref[rows, :], sin_ref[rows, :]).astype(BF16)
        kv = jnp.dot(ckv_ref[rows, :], wkv_ref[...], preferred_element_type=F32)
        k_s[rows, :MLA_NOPE_DIM] = kv[:, :MLA_NOPE_DIM].astype(BF16)
        v_s[rows, :MLA_V_DIM] = kv[:, MLA_NOPE_DIM:].astype(BF16)
    _causal_attention(lambda rows: q_s[rows, :], k_s, v_s, z_ref, o_ref, tile)


def _mla_attention(cq, ckv, w_q, w_kv, k_rope, tables, z, batch, seq, tile=ATTN_TILE):
    m = cq.shape[0]
    per_batch = lambda width: pl.BlockSpec((seq, width), lambda b, h: (b, 0))
    per_head = lambda rows: pl.BlockSpec((rows, MLA_Q_PAD), lambda b, h: (0, h))
    table = pl.BlockSpec((seq, LANES), lambda b, h: (0, 0))
    return pl.pallas_call(
        functools.partial(_mla_attention_kernel, tile=min(tile, seq)),
        grid=(batch, MLA_N_HEADS),
        in_specs=[per_batch(MLA_Q_LORA), per_batch(MLA_KV_LORA), per_head(MLA_Q_LORA), per_head(MLA_KV_LORA),
                  per_batch(LANES), table, table, pl.BlockSpec((seq, LANES), lambda b, h: (b, h))],
        out_specs=pl.BlockSpec((seq, LANES), lambda b, h: (b, h)),
        out_shape=jax.ShapeDtypeStruct((m, MLA_WIDTH), BF16),
        scratch_shapes=[pltpu.VMEM((seq, MLA_Q_PAD), BF16), pltpu.VMEM((seq, MLA_Q_PAD), BF16),
                        pltpu.VMEM((seq, 2 * MLA_V_DIM), BF16)],
        compiler_params=_params(2),
        name="mla_attention",
    )(cq, ckv, w_q, w_kv, k_rope, *tables, z)


def _forget_cum_kernel(f_ref, b_ref, o_ref):
    x = f_ref[...] + b_ref[...]
    log_f = jnp.minimum(x, 0.0) - jnp.log1p(jnp.exp(-jnp.abs(x)))
    o_ref[...] = (_cumsum_rows(log_f) * LOG2E).T


def _forget_cum(f_raw, f_bias_pad, batch, seq):
    return pl.pallas_call(
        _forget_cum_kernel,
        grid=(batch,),
        in_specs=[pl.BlockSpec((seq, LANES), lambda b: (b, 0)), pl.BlockSpec((1, LANES), lambda b: (0, 0))],
        out_specs=pl.BlockSpec((None, LANES, seq), lambda b: (b, 0, 0)),
        out_shape=jax.ShapeDtypeStruct((batch, LANES, seq), F32),
        compiler_params=_params(1),
        name="forget_cum",
    )(f_raw, f_bias_pad)


def _dilated_kernel(*refs, seq, pre_regrouped):
    n_grp = len(DIL_CONFIGS)
    qkv_refs = refs[:3 * n_grp]
    z_ref, o_ref = refs[3 * n_grp], refs[3 * n_grp + 1]
    scratch = list(refs[3 * n_grp + 2:])
    take = lambda count: [scratch.pop(0) for _ in range(count)]
    out_s = take(n_grp)
    lse_s = take(n_grp)
    stage_s = take(3)
    strided = [g for g, (_, dil) in enumerate(DIL_CONFIGS) if dil > 1]
    regroup_here = [g for g in strided if g not in pre_regrouped]
    qkv_res = {g: take(3) for g in regroup_here}
    for g in pre_regrouped:
        qkv_res[g] = qkv_refs[3 * g:3 * g + 3]
    out_res = {g: take(1)[0] for g in strided}
    lse_res = {g: take(1)[0] for g in strided}

    n_copy = 0
    for g in regroup_here:
        dil = DIL_CONFIGS[g][1]
        length = seq // dil
        for which in range(3):
            stage = stage_s[n_copy % len(stage_s)]
            n_copy += 1
            stage[...] = qkv_refs[3 * g + which][...].astype(F32)
            for residue in range(dil):
                qkv_res[g][which][residue * length:(residue + 1) * length, :] = (
                    stage[pl.ds(residue, length, stride=dil), :].astype(BF16))

    for g, (window, dil) in enumerate(DIL_CONFIGS):
        span = window // dil
        length = seq // dil
        blk = min(span, length)
        n_blk = length // blk
        q_src, k_src, v_src = qkv_res[g] if dil > 1 else qkv_refs[3 * g:3 * g + 3]
        o_dst, l_dst = (out_res[g], lse_res[g]) if dil > 1 else (out_s[g], lse_s[g])
        row1 = lax.broadcasted_iota(jnp.int32, (blk, blk), 0)
        col1 = lax.broadcasted_iota(jnp.int32, (blk, blk), 1)
        band_cur = (row1 >= col1) & (row1 - col1 <= span)
        band_prev = row1 + blk - col1 <= span
        n_batch = dil * n_blk
        as_blocks = lambda ref: ref[...].reshape(n_batch, blk, LANES)
        q3, k3, v3 = as_blocks(q_src), as_blocks(k_src), as_blocks(v_src)
        v3 = jnp.concatenate([v3, jnp.ones(v3.shape, BF16)], axis=2)
        s_cur = jnp.einsum("jqd,jkd->jqk", q3, k3, preferred_element_type=F32)
        s_cur = jnp.where(band_cur[None], s_cur, NEG_BIG)
        m = jnp.max(s_cur, axis=2, keepdims=True)
        if n_blk > 1:
            shift_blocks = lambda x3: jnp.concatenate([x3[:1], x3[:-1]], axis=0)
            k_prev, v_prev = shift_blocks(k3), shift_blocks(v3)
            block_id = lax.broadcasted_iota(jnp.int32, (n_batch, blk, blk), 0)
            s_prev = jnp.einsum("jqd,jkd->jqk", q3, k_prev, preferred_element_type=F32)
            s_prev = jnp.where(band_prev[None], s_prev, NEG_BIG)
            s_prev = jnp.where(block_id % n_blk != 0, s_prev, NEG_BIG)
            m = jnp.maximum(m, jnp.max(s_prev, axis=2, keepdims=True))
        o = jnp.einsum("jqk,jkd->jqd", jnp.exp2(s_cur - m).astype(BF16), v3, preferred_element_type=F32)
        if n_blk > 1:
            o = o + jnp.einsum("jqk,jkd->jqd", jnp.exp2(s_prev - m).astype(BF16), v_prev,
                               preferred_element_type=F32)
        den = o[:, :, LANES:]
        o_dst[...] = (o[:, :, :LANES] * (1.0 / den)).reshape(seq, LANES)
        l_dst[...] = (m + jnp.log2(den)).reshape(seq, LANES)

    for g in strided:
        dil = DIL_CONFIGS[g][1]
        length = seq // dil
        for residue in range(dil):
            src_rows = slice(residue * length, (residue + 1) * length)
            out_s[g][pl.ds(residue, length, stride=dil), :] = out_res[g][src_rows, :]
            lse_s[g][pl.ds(residue, length, stride=dil), :] = lse_res[g][src_rows, :]

    rows_per_step = min(256, seq)
    for i in range(seq // rows_per_step):
        rs = slice(i * rows_per_step, (i + 1) * rows_per_step)
        lses = [lse_s[g][rs, :] for g in range(n_grp)]
        lse_max = functools.reduce(jnp.maximum, lses)
        weights = [jnp.exp2(l - lse_max) for l in lses]
        num = sum(w * out_s[g][rs, :] for g, w in enumerate(weights))
        o = num * (1.0 / sum(weights))
        o_ref[rs, :] = (o * _silu(z_ref[rs, :].astype(F32))).astype(o_ref.dtype)


def _dilated_mixer(qkv_operands, z_operand, pre_regrouped, batch, seq):
    m = z_operand[0].shape[0]
    n_grp = len(DIL_CONFIGS)
    n_strided = sum(dil > 1 for _, dil in DIL_CONFIGS)
    n_regroup = n_strided - len(pre_regrouped)
    head_block = lambda blk0: pl.BlockSpec((seq, LANES), lambda b, h: (b, blk0 + h))
    operands = list(qkv_operands) + [z_operand]
    in_specs = [head_block(blk0) for _, blk0 in operands]
    args = [arr for arr, _ in operands]
    return pl.pallas_call(
        functools.partial(_dilated_kernel, seq=seq, pre_regrouped=tuple(pre_regrouped)),
        grid=(batch, DIL_N_HEADS),
        in_specs=in_specs,
        out_specs=pl.BlockSpec((seq, LANES), lambda b, h: (b, h)),
        out_shape=jax.ShapeDtypeStruct((m, DIL_WIDTH), BF16),
        scratch_shapes=([pltpu.VMEM((seq, LANES), F32)] * (2 * n_grp + 3)
                        + [pltpu.VMEM((seq, LANES), BF16)] * 3 * n_regroup
                        + [pltpu.VMEM((seq, LANES), F32)] * 2 * n_strided),
        compiler_params=_params(2),
        name="dilated_mixer",
    )(*args)


def _ssd_layer(u, h, next_norm_w, final, in_w, conv_w, conv_b, dt_bias, a_log, d_skip, norm_w, out_w,
               batch, seq):
    w_main = in_w[:, :SSM_MAIN_COLS].astype(BF16)
    w_dt = jnp.pad(in_w[:, SSM_MAIN_COLS:], ((0, 0), (0, LANES - SSM_N_HEADS))).astype(BF16)
    proj = _mm(u, w_main, BF16, PROJ_ROW_TILE, PROJ_COL_TILE, "ssd_in_proj")
    dt_raw = _mm(u, w_dt, F32, PROJ_ROW_TILE, PROJ_COL_TILE, "ssd_dt_proj")
    y = _ssd_mixer(proj, dt_raw, conv_w, conv_b, dt_bias, a_log, d_skip, norm_w, batch, seq)
    return _out_proj(y, out_w.astype(BF16), h, next_norm_w, final)


def _mla_layer(u, h, next_norm_w, final, in_w, q_norm_w, kv_norm_w, uq_w, ukv_w, out_w, batch, seq):
    o1 = MLA_Q_LORA
    o2 = o1 + MLA_KV_LORA
    o3 = o2 + MLA_ROPE_DIM
    rope_perm = _rope_split_perm(MLA_ROPE_DIM)
    w_kr = jnp.pad(in_w[:, o2:o3], ((0, 0), (0, LANES - MLA_ROPE_DIM)))[:, rope_perm]
    w_small = jnp.concatenate([in_w[:, :o1], w_kr, in_w[:, o1:o2]], axis=1).astype(BF16)
    w_z = in_w[:, o3:].astype(BF16)
    q_scale = MLA_QK_DIM ** -0.5 * LOG2E
    w_q = (uq_w * q_scale).reshape(MLA_Q_LORA, MLA_N_HEADS, MLA_QK_DIM)
    w_q_rope = jnp.pad(w_q[:, :, MLA_NOPE_DIM:], ((0, 0), (0, 0), (0, LANES - MLA_ROPE_DIM)))[:, :, rope_perm]
    w_q = jnp.concatenate([w_q[:, :, :MLA_NOPE_DIM], w_q_rope], axis=2).reshape(MLA_Q_LORA, -1).astype(BF16)
    w_kv = ukv_w.astype(BF16)
    tables = _rope_tables(seq, MLA_ROPE_DIM)

    cq, ckv, k_rope = _mla_small(u, w_small, q_norm_w, kv_norm_w, tables, seq)
    z = _mm(u, w_z, BF16, PROJ_ROW_TILE, PROJ_COL_TILE, "mla_gate_proj")
    o = _mla_attention(cq, ckv, w_q, w_kv, k_rope, tables, z, batch, seq)
    return _out_proj(o, out_w.astype(BF16), h, next_norm_w, final)


def _fox_layer(u, h, next_norm_w, final, in_w, f_bias, out_w, batch, seq):
    w = FOX_WIDTH
    q_scale = FOX_HEAD_DIM ** -0.5 * LOG2E
    w_main = jnp.concatenate([in_w[:, :w] * q_scale, in_w[:, w:3 * w], in_w[:, 3 * w + FOX_N_HEADS:]],
                             axis=1).astype(BF16)
    w_f = jnp.pad(in_w[:, 3 * w:3 * w + FOX_N_HEADS], ((0, 0), (0, LANES - FOX_N_HEADS))).astype(BF16)
    f_bias_pad = jnp.pad(f_bias.astype(F32), (0, LANES - FOX_N_HEADS)).reshape(1, LANES)
    proj = _mm(u, w_main, BF16, PROJ_ROW_TILE, PROJ_COL_TILE, "fox_in_proj")
    f_raw = _mm(u, w_f, F32, PROJ_ROW_TILE, LANES, "fox_forget_proj")
    cum_t = _forget_cum(f_raw, f_bias_pad, batch, seq)
    o = _fox_attention(proj, cum_t, batch, seq)
    return _out_proj(o, out_w.astype(BF16), h, next_norm_w, final)


def _dilated_layer(u, h, next_norm_w, final, in_w, out_w, batch, seq):
    w = DIL_WIDTH
    q_scale = DIL_HEAD_DIM ** -0.5 * LOG2E
    rope_perm = _rope_split_perm(DIL_ROPE_DIM)

    def split_rotary(cols):
        return cols.reshape(D_MODEL, DIL_N_HEADS, DIL_HEAD_DIM)[:, :, rope_perm].reshape(D_MODEL, w)

    n_grp = len(DIL_CONFIGS)
    qk_cols, v_cols = [], []
    for gi in range(n_grp):
        base = 3 * w * gi
        qk_cols += [split_rotary(in_w[:, base:base + w] * q_scale), split_rotary(in_w[:, base + w:base + 2 * w])]
        v_cols.append(in_w[:, base + 2 * w:base + 3 * w])
    w_gate = in_w[:, 3 * w * n_grp:]
    tables = _rope_tables(seq, DIL_ROPE_DIM)

    last = n_grp - 1
    dil = DIL_CONFIGS[last][1]

    def residue_major(a, lead):
        cols = a.shape[-1]
        return a.reshape(lead, seq // dil, dil, cols).transpose(0, 2, 1, 3).reshape(lead * seq, cols)

    u_res = residue_major(u, batch)
    tables_res = tuple(residue_major(t, 1) for t in tables)
    w_qk = jnp.concatenate(qk_cols[:2 * last], axis=1).astype(BF16)
    w_qk_last = jnp.concatenate(qk_cols[2 * last:], axis=1).astype(BF16)
    w_vz = jnp.concatenate(v_cols[:last] + [w_gate], axis=1).astype(BF16)
    qk = _mm_rope(u, w_qk, tables, seq, BF16, PROJ_ROW_TILE, PROJ_COL_TILE, "dilated_qk_proj")
    qk_last = _mm_rope(u_res, w_qk_last, tables_res, seq, BF16, PROJ_ROW_TILE, PROJ_COL_TILE,
                       "dilated_qk_proj_regrouped")
    vz = _mm(u, w_vz, BF16, PROJ_ROW_TILE, PROJ_COL_TILE, "dilated_vz_proj")
    v_last = _mm(u_res, v_cols[last].astype(BF16), BF16, PROJ_ROW_TILE, PROJ_COL_TILE, "dilated_v_proj_regrouped")
    hb = DIL_N_HEADS
    operands = []
    for gi in range(last):
        operands += [(qk, 2 * hb * gi), (qk, 2 * hb * gi + hb), (vz, hb * gi)]
    operands += [(qk_last, 0), (qk_last, hb), (v_last, 0)]
    o = _dilated_mixer(operands, (vz, hb * last), (last,), batch, seq)
    return _out_proj(o, out_w.astype(BF16), h, next_norm_w, final)


def kernel(x, norm_w, final_norm_w, ssm_in_w, ssm_conv_w, ssm_conv_b, ssm_dt_bias, ssm_A_log, ssm_D,
           ssm_norm_w, ssm_out_w, mla_in_w, mla_q_norm_w, mla_kv_norm_w, mla_uq_w, mla_ukv_w, mla_out_w,
           fox_in_w, fox_f_bias, fox_out_w, dil_in_w, dil_out_w):
    batch, seq, d = x.shape
    depth = norm_w.shape[0]
    h = x.reshape(batch * seq, d)
    u = _rmsnorm(h, norm_w[0], BF16)
    for i in range(depth):
        kind, j = i % 4, i // 4
        final = i == depth - 1
        next_w = final_norm_w if final else norm_w[i + 1]
        if kind == 0:
            res = _ssd_layer(u, h, next_w, final, ssm_in_w[j], ssm_conv_w[j], ssm_conv_b[j], ssm_dt_bias[j],
                             ssm_A_log[j], ssm_D[j], ssm_norm_w[j], ssm_out_w[j], batch, seq)
        elif kind == 1:
            res = _mla_layer(u, h, next_w, final, mla_in_w[j], mla_q_norm_w[j], mla_kv_norm_w[j], mla_uq_w[j],
                             mla_ukv_w[j], mla_out_w[j], batch, seq)
        elif kind == 2:
            res = _fox_layer(u, h, next_w, final, fox_in_w[j], fox_f_bias[j], fox_out_w[j], batch, seq)
        else:
            res = _dilated_layer(u, h, next_w, final, dil_in_w[j], dil_out_w[j], batch, seq)
        if final:
            return res.reshape(batch, seq, d)
        h, u = res
```
